```python
import jax, jax.numpy as jnp
from jax import lax
import numpy as np

D_MODEL = 1024
BATCH = 8
SEQ = 2048
DEPTH = 4
DEC_BATCH = 128
DEC_SEQ = 4
PAST_LEN = 16384
PAGE_SIZE = 128

W_A = D_MODEL
W_B = D_MODEL
W_C = D_MODEL
N_BRANCH = 3
CONV_A_WIDTH = 31
CONV_B_WIDTH = 4
CONV_C_WIDTH = 3
LRU_BLOCKS = 16
LRU_BLOCK = W_B // LRU_BLOCKS
LRU_C = 8.0
EPS = 1e-6
SPLIT_SIZES = (W_A, W_A, W_A, W_B, W_B, W_C, W_C, W_C, W_C, N_BRANCH * D_MODEL)
IN_COLS = sum(SPLIT_SIZES)
SPLIT_POINTS = tuple(int(s) for s in np.cumsum(SPLIT_SIZES)[:-1])

kernel_name = "hybrid_gated_conformer_rglru_shortconv_step"


def rmsnorm(x, g):
    xf = x.astype(jnp.float32)
    y = xf * lax.rsqrt(jnp.mean(xf * xf, axis=-1, keepdims=True) + EPS)
    return (y * g.astype(jnp.float32)).astype(x.dtype)


def layernorm(x, g, b):
    xf = x.astype(jnp.float32)
    mu = jnp.mean(xf, axis=-1, keepdims=True)
    var = jnp.mean(jnp.square(xf - mu), axis=-1, keepdims=True)
    y = (xf - mu) * lax.rsqrt(var + EPS)
    return (y * g.astype(jnp.float32) + b.astype(jnp.float32)).astype(x.dtype)


def causal_dwconv(u, buf, w):
    k = w.shape[0]
    full = jnp.concatenate([buf.astype(u.dtype), u], axis=1)
    out = lax.conv_general_dilated(full, w.astype(u.dtype)[:, None, :], window_strides=(1,),
                                   padding='VALID', dimension_numbers=('NWC', 'WIO', 'NWC'),
                                   feature_group_count=u.shape[-1])
    new_buf = full[:, full.shape[1] - (k - 1):]
    return out, new_buf


def rglru(x, h0, w_a, b_a, w_x, b_x, lam):
    bsz, t, w = x.shape
    xh = x.reshape(bsz, t, LRU_BLOCKS, LRU_BLOCK)
    r = jax.nn.sigmoid(jnp.einsum('bthi,hij->bthj', xh, w_a).reshape(bsz, t, w) + b_a)
    i = jax.nn.sigmoid(jnp.einsum('bthi,hij->bthj', xh, w_x).reshape(bsz, t, w) + b_x)
    log_a = -LRU_C * r.astype(jnp.float32) * jax.nn.softplus(-lam.astype(jnp.float32))
    a = jnp.exp(log_a)
    u = jnp.sqrt(-jnp.expm1(2.0 * log_a)) * (i * x).astype(jnp.float32)

    def step(h, inp):
        a_t, u_t = inp
        h = a_t * h + u_t
        return h, h

    h_last, hs = lax.scan(step, h0.astype(jnp.float32), (jnp.swapaxes(a, 0, 1), jnp.swapaxes(u, 0, 1)))
    return jnp.swapaxes(hs, 0, 1).astype(x.dtype), h_last.astype(h0.dtype)


def hybrid_layer(x, buf_a, buf_b, h_b, buf_c, p):
    bsz, t, _ = x.shape
    h = rmsnorm(x, p['norm_pre'])
    proj = jnp.einsum('btd,dc->btc', h, p['w_in']) + p['b_in']
    a_v, a_g, a_z, b_x, b_z, c_b, c_c, c_x, c_z, gates = jnp.split(proj, SPLIT_POINTS, axis=-1)

    ua = a_v * jax.nn.sigmoid(a_g)
    ua, nbuf_a = causal_dwconv(ua, buf_a, p['conv_a_w'])
    ua = jax.nn.silu(layernorm(ua + p['conv_a_b'], p['ln_a_g'], p['ln_a_b']))
    y_a = jnp.einsum('btc,cd->btd', ua * jax.nn.silu(a_z), p['w_a_out'])

    xb, nbuf_b = causal_dwconv(b_x, buf_b, p['conv_b_w'])
    xb = xb + p['conv_b_b']
    yb, nh_b = rglru(xb, h_b, p['rg_w_a'], p['rg_b_a'], p['rg_w_x'], p['rg_b_x'], p['rg_lambda'])
    y_b = jnp.einsum('btc,cd->btd', yb * jax.nn.silu(b_z), p['w_b_out'])

    cx, nbuf_c = causal_dwconv(c_c * c_x, buf_c, p['conv_c_w'])
    y_c = jnp.einsum('btc,cd->btd', c_b * cx * jax.nn.silu(c_z), p['w_c_out'])

    g = jax.nn.sigmoid(gates).reshape(bsz, t, N_BRANCH, D_MODEL)
    merged = g[:, :, 0] * y_a + g[:, :, 1] * y_b + g[:, :, 2] * y_c
    out = jnp.einsum('btd,de->bte', merged, p['w_o'])
    return x + rmsnorm(out, p['norm_post']), nbuf_a, nbuf_b, nh_b, nbuf_c


def setup_inputs(seed: int = 0) -> dict:
    key = jax.random.key(seed)
    ks = jax.random.split(key, 32)

    def nrm(k, shape, scale):
        return jax.random.normal(k, shape, jnp.float32) * scale

    a0 = jax.random.uniform(ks[22], (DEPTH, W_B), jnp.float32, 0.9, 0.999)
    return {
        'x_prompt': nrm(ks[0], (BATCH, SEQ, D_MODEL), 1.0),
        'x_sample': nrm(ks[1], (DEC_BATCH, DEC_SEQ, D_MODEL), 1.0),
        'state_conv_a': nrm(ks[2], (DEPTH, DEC_BATCH, CONV_A_WIDTH - 1, W_A), 0.5),
        'state_conv_b': nrm(ks[3], (DEPTH, DEC_BATCH, CONV_B_WIDTH - 1, W_B), 1.0),
        'state_lru': nrm(ks[4], (DEPTH, DEC_BATCH, W_B), 0.5),
        'state_conv_c': nrm(ks[5], (DEPTH, DEC_BATCH, CONV_C_WIDTH - 1, W_C), 0.5),
        'norm_pre': 1.0 + nrm(ks[6], (DEPTH, D_MODEL), 0.02),
        'norm_post': 1.0 + nrm(ks[7], (DEPTH, D_MODEL), 0.02),
        'w_in': nrm(ks[8], (DEPTH, D_MODEL, IN_COLS), D_MODEL ** -0.5),
        'b_in': nrm(ks[9], (DEPTH, IN_COLS), 0.02),
        'conv_a_w': nrm(ks[10], (DEPTH, CONV_A_WIDTH, W_A), CONV_A_WIDTH ** -0.5),
        'conv_a_b': nrm(ks[11], (DEPTH, W_A), 0.02),
        'ln_a_g': 1.0 + nrm(ks[12], (DEPTH, W_A), 0.02),
        'ln_a_b': nrm(ks[13], (DEPTH, W_A), 0.02),
        'w_a_out': nrm(ks[14], (DEPTH, W_A, D_MODEL), W_A ** -0.5),
        'conv_b_w': nrm(ks[15], (DEPTH, CONV_B_WIDTH, W_B), CONV_B_WIDTH ** -0.5),
        'conv_b_b': nrm(ks[16], (DEPTH, W_B), 0.02),
        'rg_w_a': nrm(ks[17], (DEPTH, LRU_BLOCKS, LRU_BLOCK, LRU_BLOCK), LRU_BLOCK ** -0.5),
        'rg_b_a': nrm(ks[18], (DEPTH, W_B), 0.02),
        'rg_w_x': nrm(ks[19], (DEPTH, LRU_BLOCKS, LRU_BLOCK, LRU_BLOCK), LRU_BLOCK ** -0.5),
        'rg_b_x': nrm(ks[20], (DEPTH, W_B), 0.02),
        'rg_lambda': jnp.log(a0) - jnp.log1p(-a0),
        'w_b_out': nrm(ks[21], (DEPTH, W_B, D_MODEL), W_B ** -0.5),
        'conv_c_w': nrm(ks[23], (DEPTH, CONV_C_WIDTH, W_C), CONV_C_WIDTH ** -0.5),
        'w_c_out': nrm(ks[24], (DEPTH, W_C, D_MODEL), W_C ** -0.5),
        'w_o': nrm(ks[25], (DEPTH, D_MODEL, D_MODEL), D_MODEL ** -0.5),
    }


def reference(x_prompt, x_sample, state_conv_a, state_conv_b, state_lru, state_conv_c,
              norm_pre, norm_post, w_in, b_in, conv_a_w, conv_a_b, ln_a_g, ln_a_b, w_a_out,
              conv_b_w, conv_b_b, rg_w_a, rg_b_a, rg_w_x, rg_b_x, rg_lambda, w_b_out,
              conv_c_w, w_c_out, w_o):
    bp = x_prompt.shape[0]
    dt = x_prompt.dtype
    xp, xs = x_prompt, x_sample
    pa, pb, ph, pc = [], [], [], []
    sa, sb, sh, sc = [], [], [], []
    for l in range(DEPTH):
        p = {
            'norm_pre': norm_pre[l], 'norm_post': norm_post[l], 'w_in': w_in[l], 'b_in': b_in[l],
            'conv_a_w': conv_a_w[l], 'conv_a_b': conv_a_b[l], 'ln_a_g': ln_a_g[l], 'ln_a_b': ln_a_b[l],
            'w_a_out': w_a_out[l], 'conv_b_w': conv_b_w[l], 'conv_b_b': conv_b_b[l],
            'rg_w_a': rg_w_a[l], 'rg_b_a': rg_b_a[l], 'rg_w_x': rg_w_x[l], 'rg_b_x': rg_b_x[l],
            'rg_lambda': rg_lambda[l], 'w_b_out': w_b_out[l], 'conv_c_w': conv_c_w[l],
            'w_c_out': w_c_out[l], 'w_o': w_o[l],
        }
        xp, na, nb, nh, nc = hybrid_layer(
            xp,
            jnp.zeros((bp, CONV_A_WIDTH - 1, W_A), dt),
            jnp.zeros((bp, CONV_B_WIDTH - 1, W_B), dt),
            jnp.zeros((bp, W_B), dt),
            jnp.zeros((bp, CONV_C_WIDTH - 1, W_C), dt),
            p)
        pa.append(na); pb.append(nb); ph.append(nh); pc.append(nc)
        xs, na, nb, nh, nc = hybrid_layer(
            xs, state_conv_a[l], state_conv_b[l], state_lru[l], state_conv_c[l], p)
        sa.append(na); sb.append(nb); sh.append(nh); sc.append(nc)
    return (xp, xs,
            jnp.stack(pa), jnp.stack(pb), jnp.stack(ph), jnp.stack(pc),
            jnp.stack(sa), jnp.stack(sb), jnp.stack(sh), jnp.stack(sc))
```

```python
import functools

import jax
import jax.numpy as jnp
from jax import lax
from jax.experimental import pallas as pl
from jax.experimental.pallas import tpu as pltpu

D = 1024
DEPTH = 4
EPS = 1e-6
LRU_C = 8.0
KA, KB, KC = 31, 4, 3
N_RG_GROUPS = 4
RG_W = D // N_RG_GROUPS

SUBLANES = 8
LANES = 128
N_LANE_TILES = D // LANES

C_AV, C_AG, C_AZ, C_BX, C_BZ, C_CB, C_CC, C_CX, C_CZ, C_G0, C_G1, C_G2 = range(12)
N_COL_GROUPS = 12

V_NORM_PRE, V_NORM_POST, V_CONV_A_B, V_LN_G, V_LN_B, V_CONV_B_B, V_RG_B_A, V_RG_B_X, V_LAMBDA = range(9)
V_B_IN = 9
V_CONV_A_W = V_B_IN + N_COL_GROUPS
V_CONV_B_W = V_CONV_A_W + KA
V_CONV_C_W = V_CONV_B_W + KB
NV = 64
assert V_CONV_C_W + KC <= NV

TM = 256
HIST_A = 32
HIST_S = 8
RC = 16
SAMPLE_BLOCK = 32
VMEM_LIMIT_BYTES = 58 * 1024 * 1024


def _sigmoid(x):
    return jax.nn.sigmoid(x)


def _silu(x):
    return x * jax.nn.sigmoid(x)


def _for_chunks(n_rows, rc, body):
    def wrapped(i, carry):
        body(pl.multiple_of(i * rc, rc))
        return carry
    lax.fori_loop(0, n_rows // rc, wrapped, 0)


def _cols(g, n=1):
    return slice(g * D, (g + n) * D)


class _Refs:
    def __init__(self, vec, w_in, w_a, w_b, w_c, w_o, w_rga, w_rgx, hb, p, q, z, yacc, wexp):
        self.vec, self.w_in, self.w_a, self.w_b, self.w_c, self.w_o = vec, w_in, w_a, w_b, w_c, w_o
        self.w_rga, self.w_rgx = w_rga, w_rgx
        self.hb, self.p, self.q, self.z, self.yacc, self.wexp = hb, p, q, z, yacc, wexp

    def row(self, i):
        return self.vec[i:i + 1, :]

    def bias(self, g):
        return self.vec[V_B_IN + g:V_B_IN + g + 1, :]


def _expand_conv_a_weights(r):
    for k in range(KA):
        r.wexp[k * SUBLANES:(k + 1) * SUBLANES, :] = jnp.broadcast_to(
            r.row(V_CONV_A_W + k), (SUBLANES, D))


def _in_proj(r, n_rows, g, dst_g):
    r.p[0:n_rows, _cols(dst_g)] = jnp.dot(
        r.hb[0:n_rows, :], r.w_in[:, _cols(g)], preferred_element_type=jnp.float32)


def _layer_body(r, n_rows, x_load, y_store, glu_store, conv_a, bx_store, conv_b, scan, cc_store, conv_c):
    f32, bf16 = jnp.float32, jnp.bfloat16

    g_pre = r.row(V_NORM_PRE)

    def norm_body(r0):
        rows = pl.ds(r0, RC)
        x = x_load(r0)
        ms = jnp.mean(x * x, axis=-1, keepdims=True)
        r.hb[rows, :] = ((x * lax.rsqrt(ms + EPS)) * g_pre).astype(bf16)
    _for_chunks(n_rows, RC, norm_body)

    def merge(first, last):
        def body(r0, bias):
            rows = pl.ds(r0, RC)
            contrib = _sigmoid(r.p[rows, _cols(1)] + bias) * r.p[rows, _cols(0)]
            if not first:
                contrib = r.yacc[rows, :] + contrib
            if last:
                r.z[rows, :] = contrib.astype(bf16)
            else:
                r.yacc[rows, :] = contrib
        return body

    def branch_out(w_ref, gate_g, first=False, last=False):
        r.p[0:n_rows, _cols(0)] = jnp.dot(r.z[0:n_rows, :], w_ref[...], preferred_element_type=f32)
        _in_proj(r, n_rows, gate_g, 1)
        bias = r.bias(gate_g)
        body = merge(first, last)
        _for_chunks(n_rows, RC, lambda r0: body(r0, bias))

    for j, g in enumerate((C_AV, C_AG, C_AZ)):
        _in_proj(r, n_rows, g, j)
    b_av, b_ag, b_az = r.bias(C_AV), r.bias(C_AG), r.bias(C_AZ)

    def glu_body(r0):
        rows = pl.ds(r0, RC)
        glu_store(r0, (r.p[rows, _cols(0)] + b_av) * _sigmoid(r.p[rows, _cols(1)] + b_ag))
    _for_chunks(n_rows, RC, glu_body)

    conv_a()
    conv_a_b, ln_g, ln_b = r.row(V_CONV_A_B), r.row(V_LN_G), r.row(V_LN_B)

    def ln_body(r0):
        rows = pl.ds(r0, RC)
        v = r.p[rows, _cols(0)] + conv_a_b
        mu = jnp.mean(v, axis=-1, keepdims=True)
        d = v - mu
        var = jnp.mean(d * d, axis=-1, keepdims=True)
        y = (d * lax.rsqrt(var + EPS)) * ln_g + ln_b
        r.z[rows, :] = (_silu(y) * _silu(r.p[rows, _cols(2)] + b_az)).astype(bf16)
    _for_chunks(n_rows, RC, ln_body)
    branch_out(r.w_a, C_G0, first=True)

    for j, g in enumerate((C_BX, C_BZ)):
        _in_proj(r, n_rows, g, j)
    b_bx, b_bz = r.bias(C_BX), r.bias(C_BZ)

    def bx_body(r0):
        bx_store(r0, r.p[pl.ds(r0, RC), _cols(0)] + b_bx)
    _for_chunks(n_rows, RC, bx_body)

    conv_b()

    def xb_cast_body(r0):
        rows = pl.ds(r0, RC)
        r.z[rows, :] = r.p[rows, _cols(2)].astype(bf16)
    _for_chunks(n_rows, RC, xb_cast_body)

    for g in range(N_RG_GROUPS):
        lanes = slice(g * RG_W, (g + 1) * RG_W)
        xg = r.z[0:n_rows, lanes]
        r.q[0:n_rows, lanes] = jnp.dot(xg, r.w_rga[g], preferred_element_type=f32)
        r.q[0:n_rows, slice(D + g * RG_W, D + (g + 1) * RG_W)] = jnp.dot(
            xg, r.w_rgx[g], preferred_element_type=f32)

    neg_lam = -r.row(V_LAMBDA)
    softplus_neg_lam = jnp.maximum(neg_lam, 0.0) + jnp.log1p(jnp.exp(-jnp.abs(neg_lam)))
    rg_b_a, rg_b_x = r.row(V_RG_B_A), r.row(V_RG_B_X)

    def rg_body(r0):
        rows = pl.ds(r0, RC)
        rgate = _sigmoid(r.q[rows, _cols(0)] + rg_b_a)
        igate = _sigmoid(r.q[rows, _cols(1)] + rg_b_x)
        log_a = (-LRU_C * rgate) * softplus_neg_lam
        a = jnp.exp(log_a)
        mult = jnp.sqrt(-jnp.tanh(log_a) * (1.0 + a * a))
        r.q[rows, _cols(0)] = a
        r.q[rows, _cols(1)] = mult * (igate * r.p[rows, _cols(2)])
    _for_chunks(n_rows, RC, rg_body)

    scan()

    def bz_body(r0):
        rows = pl.ds(r0, RC)
        r.z[rows, :] = (r.q[rows, _cols(0)] * _silu(r.p[rows, _cols(1)] + b_bz)).astype(bf16)
    _for_chunks(n_rows, RC, bz_body)
    branch_out(r.w_b, C_G1)

    for j, g in enumerate((C_CB, C_CC, C_CX, C_CZ)):
        _in_proj(r, n_rows, g, j)
    b_cb, b_cc, b_cx, b_cz = r.bias(C_CB), r.bias(C_CC), r.bias(C_CX), r.bias(C_CZ)

    def cc_body(r0):
        rows = pl.ds(r0, RC)
        cc_store(r0, (r.p[rows, _cols(1)] + b_cc) * (r.p[rows, _cols(2)] + b_cx))
    _for_chunks(n_rows, RC, cc_body)

    conv_c()

    def cz_body(r0):
        rows = pl.ds(r0, RC)
        zc = ((r.p[rows, _cols(0)] + b_cb) * r.p[rows, _cols(1)]) * _silu(r.p[rows, _cols(3)] + b_cz)
        r.z[rows, :] = zc.astype(bf16)
    _for_chunks(n_rows, RC, cz_body)
    branch_out(r.w_c, C_G2, last=True)

    r.p[0:n_rows, _cols(0)] = jnp.dot(r.z[0:n_rows, :], r.w_o[...], preferred_element_type=f32)
    g_post = r.row(V_NORM_POST)

    def out_body(r0):
        rows = pl.ds(r0, RC)
        o = r.p[rows, _cols(0)]
        ms = jnp.mean(o * o, axis=-1, keepdims=True)
        y_store(r0, x_load(r0) + (o * lax.rsqrt(ms + EPS)) * g_post)
    _for_chunks(n_rows, RC, out_body)


def _row_shift_conv(src_ref, hist_rows, n_blocks, k_width, tap_weight, store):
    nq = -(-k_width // SUBLANES)
    shifts = [s for s in range(1, SUBLANES) if s < k_width]
    sub = lax.broadcasted_iota(jnp.int32, (SUBLANES, LANES), 0)

    for c in range(N_LANE_TILES):
        lanes = slice(c * LANES, (c + 1) * LANES)

        def qsum(s, blocks):
            acc = None
            for qi in range(nq):
                j = SUBLANES * qi + s
                if j >= k_width:
                    break
                term = tap_weight(j, lanes) * blocks[qi]
                acc = term if acc is None else acc + term
            return acc

        hist = [src_ref[hist_rows - SUBLANES * (qi + 1):hist_rows - SUBLANES * qi, lanes]
                for qi in range(nq)]
        carry0 = (tuple(hist[:nq - 1]), tuple(qsum(s, hist) for s in shifts))

        def body(m, carry):
            prevs, qprev = carry
            r0 = pl.multiple_of(m * SUBLANES, SUBLANES)
            um = src_ref[pl.ds(hist_rows + r0, SUBLANES), lanes]
            blocks = (um,) + prevs
            out = qsum(0, blocks)
            qs = []
            for idx, s in enumerate(shifts):
                qr = qsum(s, blocks)
                out = out + pltpu.roll(jnp.where(sub < SUBLANES - s, qr, qprev[idx]), s, 0)
                qs.append(qr)
            store(r0, lanes, out)
            return (blocks[:nq - 1], tuple(qs))

        lax.fori_loop(0, n_blocks, body, carry0)


def _prompt_kernel(x_ref, vec_ref, w_in_ref, w_a_ref, w_b_ref, w_c_ref, w_o_ref, w_rga_ref, w_rgx_ref,
                   y_ref, na_ref, nb_ref, nh_ref, nc_ref,
                   hb_ref, p_ref, q_ref, z_ref, yacc_ref, wexp_ref, ua_ref, bx_ref, cc_ref, h_ref):
    t = pl.program_id(1)
    r = _Refs(vec_ref, w_in_ref, w_a_ref, w_b_ref, w_c_ref, w_o_ref, w_rga_ref, w_rgx_ref,
              hb_ref, p_ref, q_ref, z_ref, yacc_ref, wexp_ref)
    n_blocks = TM // SUBLANES

    @pl.when(t == 0)
    def _():
        ua_ref[0:HIST_A, :] = jnp.zeros((HIST_A, D), jnp.float32)
        bx_ref[0:HIST_S, :] = jnp.zeros((HIST_S, D), jnp.float32)
        cc_ref[0:HIST_S, :] = jnp.zeros((HIST_S, D), jnp.float32)
        h_ref[...] = jnp.zeros((SUBLANES, D), jnp.float32)
        _expand_conv_a_weights(r)

    def hist_store(ref, hist_rows):
        def store(r0, val):
            ref[pl.ds(hist_rows + r0, RC), :] = val
        return store

    def conv_a():
        def w(j, lanes):
            k = KA - 1 - j
            return wexp_ref[k * SUBLANES:(k + 1) * SUBLANES, lanes]

        def store(r0, lanes, out):
            p_ref[pl.ds(r0, SUBLANES), lanes] = out
        _row_shift_conv(ua_ref, HIST_A, n_blocks, KA, w, store)

    def small_conv(src_ref, v_row, k_width, dst_g, bias_row):
        def run():
            def w(j, lanes):
                k = k_width - 1 - j
                return jnp.broadcast_to(vec_ref[v_row + k:v_row + k + 1, lanes], (SUBLANES, LANES))

            def store(r0, lanes, out):
                if bias_row is not None:
                    out = out + vec_ref[bias_row:bias_row + 1, lanes]
                p_ref[pl.ds(r0, SUBLANES), slice(dst_g * D + lanes.start, dst_g * D + lanes.stop)] = out
            _row_shift_conv(src_ref, HIST_S, n_blocks, k_width, w, store)
        return run

    def scan():
        sub = lax.broadcasted_iota(jnp.int32, (SUBLANES, D), 0)

        def body(m, h):
            rows = pl.ds(pl.multiple_of(m * SUBLANES, SUBLANES), SUBLANES)
            a = q_ref[rows, _cols(0)]
            u = q_ref[rows, _cols(1)]
            for s in (1, 2, 4):
                keep = sub >= s
                a_sh = pltpu.roll(a, s, 0)
                u_sh = pltpu.roll(u, s, 0)
                u = jnp.where(keep, a * u_sh + u, u)
                a = jnp.where(keep, a * a_sh, a)
            hs = a * h + u
            q_ref[rows, _cols(0)] = hs
            return jnp.broadcast_to(hs[SUBLANES - 1:SUBLANES, :], (SUBLANES, D))

        h_ref[...] = lax.fori_loop(0, n_blocks, body, h_ref[...])

    _layer_body(
        r, TM,
        x_load=lambda r0: x_ref[pl.ds(r0, RC), :],
        y_store=lambda r0, v: y_ref.__setitem__((pl.ds(r0, RC), slice(None)), v),
        glu_store=hist_store(ua_ref, HIST_A), conv_a=conv_a,
        bx_store=hist_store(bx_ref, HIST_S), conv_b=small_conv(bx_ref, V_CONV_B_W, KB, 2, V_CONV_B_B),
        scan=scan,
        cc_store=hist_store(cc_ref, HIST_S), conv_c=small_conv(cc_ref, V_CONV_C_W, KC, 1, None))

    @pl.when(t == pl.num_programs(1) - 1)
    def _():
        na_ref[...] = ua_ref[HIST_A + TM - (KA - 1):HIST_A + TM, :]
        nb_ref[...] = bx_ref[HIST_S + TM - (KB - 1):HIST_S + TM, :]
        nc_ref[...] = cc_ref[HIST_S + TM - (KC - 1):HIST_S + TM, :]
        nh_ref[...] = h_ref[0:1, :]

    ua_ref[0:HIST_A, :] = ua_ref[TM:TM + HIST_A, :]
    bx_ref[0:HIST_S, :] = bx_ref[TM:TM + HIST_S, :]
    cc_ref[0:HIST_S, :] = cc_ref[TM:TM + HIST_S, :]


def _resident(shape, layer):
    nd = len(shape)
    return pl.BlockSpec((None,) + tuple(shape[1:]), lambda *_: (layer,) + (0,) * (nd - 1),
                        pipeline_mode=pl.Buffered(1))


def _weight_specs(weights, layer):
    return [_resident(w.shape, layer) for w in weights]


def _shared_scratch(n_rows):
    return [
        pltpu.VMEM((n_rows, D), jnp.bfloat16),
        pltpu.VMEM((n_rows, 4 * D), jnp.float32),
        pltpu.VMEM((n_rows, 2 * D), jnp.float32),
        pltpu.VMEM((n_rows, D), jnp.bfloat16),
        pltpu.VMEM((n_rows, D), jnp.float32),
        pltpu.VMEM((KA * SUBLANES, D), jnp.float32),
    ]


def _prompt_layer(x, weights, layer):
    bsz, seq, _ = x.shape
    assert seq % TM == 0 and TM % RC == 0
    f32 = jnp.float32
    out_shape = (
        jax.ShapeDtypeStruct((bsz, seq, D), f32),
        jax.ShapeDtypeStruct((bsz, KA - 1, D), f32),
        jax.ShapeDtypeStruct((bsz, KB - 1, D), f32),
        jax.ShapeDtypeStruct((bsz, 1, D), f32),
        jax.ShapeDtypeStruct((bsz, KC - 1, D), f32),
    )
    per_seq = lambda rows: pl.BlockSpec((None, rows, D), lambda b, t: (b, 0, 0))
    return pl.pallas_call(
        _prompt_kernel,
        grid=(bsz, seq // TM),
        in_specs=[pl.BlockSpec((None, TM, D), lambda b, t: (b, t, 0))] + _weight_specs(weights, layer),
        out_specs=(pl.BlockSpec((None, TM, D), lambda b, t: (b, t, 0)),
                   per_seq(KA - 1), per_seq(KB - 1), per_seq(1), per_seq(KC - 1)),
        out_shape=out_shape,
        scratch_shapes=_shared_scratch(TM) + [
            pltpu.VMEM((HIST_A + TM, D), f32),
            pltpu.VMEM((HIST_S + TM, D), f32),
            pltpu.VMEM((HIST_S + TM, D), f32),
            pltpu.VMEM((SUBLANES, D), f32),
        ],
        compiler_params=pltpu.CompilerParams(
            dimension_semantics=("arbitrary", "arbitrary"), vmem_limit_bytes=VMEM_LIMIT_BYTES),
        name=f"prompt_layer{layer}",
    )(x, *weights)


def _sample_kernel(n_t, x_ref, sa_ref, sb_ref, sh_ref, sc_ref,
                   vec_ref, w_in_ref, w_a_ref, w_b_ref, w_c_ref, w_o_ref, w_rga_ref, w_rgx_ref,
                   y_ref, ua_out_ref, nb_ref, nh_ref, nc_ref,
                   hb_ref, p_ref, q_ref, z_ref, yacc_ref, wexp_ref, ua_ref, bx_ref, cc_ref):
    bs = SAMPLE_BLOCK
    n_rows = n_t * bs
    r = _Refs(vec_ref, w_in_ref, w_a_ref, w_b_ref, w_c_ref, w_o_ref, w_rga_ref, w_rgx_ref,
              hb_ref, p_ref, q_ref, z_ref, yacc_ref, wexp_ref)
    _expand_conv_a_weights(r)

    def flat_store(ref):
        def store(r0, val):
            ref[pl.ds(r0, RC), :] = val
        return store

    def slab_conv(state_ref, new_ref, k_width, tap_weight, dst_g, bias_row):
        n_state = k_width - 1

        def run():
            def body(r0):
                rows = pl.ds(r0, SUBLANES)
                for t in range(n_t):
                    acc = None
                    for k in range(k_width):
                        i = t + k
                        if i < n_state:
                            full = state_ref[i, rows, :]
                        else:
                            full = new_ref[pl.ds((i - n_state) * bs + r0, SUBLANES), :]
                        term = tap_weight(k) * full
                        acc = term if acc is None else acc + term
                    if bias_row is not None:
                        acc = acc + r.row(bias_row)
                    p_ref[pl.ds(t * bs + r0, SUBLANES), _cols(dst_g)] = acc
            _for_chunks(bs, SUBLANES, body)
        return run

    def vec_tap(v_row):
        return lambda k: r.row(v_row + k)

    def scan():
        def body(r0):
            rows = pl.ds(r0, SUBLANES)
            h = sh_ref[rows, :]
            for t in range(n_t):
                trows = pl.ds(t * bs + r0, SUBLANES)
                h = q_ref[trows, _cols(0)] * h + q_ref[trows, _cols(1)]
                q_ref[trows, _cols(0)] = h
            nh_ref[rows, :] = h
        _for_chunks(bs, SUBLANES, body)

    _layer_body(
        r, n_rows,
        x_load=lambda r0: x_ref[pl.ds(r0, RC), :],
        y_store=lambda r0, v: y_ref.__setitem__((pl.ds(r0, RC), slice(None)), v),
        glu_store=flat_store(ua_ref),
        conv_a=slab_conv(sa_ref, ua_ref, KA,
                         lambda k: wexp_ref[k * SUBLANES:(k + 1) * SUBLANES, :], 0, None),
        bx_store=flat_store(bx_ref),
        conv_b=slab_conv(sb_ref, bx_ref, KB, vec_tap(V_CONV_B_W), 2, V_CONV_B_B),
        scan=scan,
        cc_store=flat_store(cc_ref),
        conv_c=slab_conv(sc_ref, cc_ref, KC, vec_tap(V_CONV_C_W), 1, None))

    ua_out_ref[...] = ua_ref[...]
    nb_ref[...] = bx_ref[(n_t - (KB - 1)) * bs:n_t * bs, :]
    nc_ref[...] = cc_ref[(n_t - (KC - 1)) * bs:n_t * bs, :]


def _sample_layer(x_tm, sa_tm, sb_tm, sh, sc_tm, weights, layer):
    n_blk, n_rows, _ = x_tm.shape
    bs = SAMPLE_BLOCK
    n_t = n_rows // bs
    bsz = n_blk * bs
    assert n_t >= KB - 1 and n_t >= KC - 1 and bs % RC == 0
    f32 = jnp.float32
    state = lambda k: pl.BlockSpec((None, k, bs, D), lambda i: (layer, 0, i, 0))
    rows_blk = lambda n: pl.BlockSpec((None, n, D), lambda i: (i, 0, 0))
    out_shape = (
        jax.ShapeDtypeStruct((n_blk, n_rows, D), f32),
        jax.ShapeDtypeStruct((n_blk, n_rows, D), f32),
        jax.ShapeDtypeStruct((n_blk, (KB - 1) * bs, D), f32),
        jax.ShapeDtypeStruct((bsz, D), f32),
        jax.ShapeDtypeStruct((n_blk, (KC - 1) * bs, D), f32),
    )
    return pl.pallas_call(
        functools.partial(_sample_kernel, n_t),
        grid=(n_blk,),
        in_specs=[rows_blk(n_rows), state(KA - 1), state(KB - 1),
                  pl.BlockSpec((None, bs, D), lambda i: (layer, i, 0)), state(KC - 1)]
        + _weight_specs(weights, layer),
        out_specs=(rows_blk(n_rows), rows_blk(n_rows), rows_blk((KB - 1) * bs),
                   pl.BlockSpec((bs, D), lambda i: (i, 0)), rows_blk((KC - 1) * bs)),
        out_shape=out_shape,
        scratch_shapes=_shared_scratch(n_rows) + [
            pltpu.VMEM((n_rows, D), f32),
            pltpu.VMEM((n_rows, D), f32),
            pltpu.VMEM((n_rows, D), f32),
        ],
        compiler_params=pltpu.CompilerParams(
            dimension_semantics=("arbitrary",), vmem_limit_bytes=VMEM_LIMIT_BYTES),
        name=f"sample_layer{layer}",
    )(x_tm, sa_tm, sb_tm, sh, sc_tm, *weights)


def _block_diag_groups(w):
    depth, n_blocks, blk, _ = w.shape
    per = n_blocks // N_RG_GROUPS
    w = w.reshape(depth, N_RG_GROUPS, per, blk, blk)
    eye = jnp.eye(per, dtype=w.dtype)
    dense = jnp.einsum('lgaij,ab->lgaibj', w, eye)
    return dense.reshape(depth, N_RG_GROUPS, per * blk, per * blk)


def _to_time_major_blocks(x):
    bsz, n_t, _ = x.shape
    bs = SAMPLE_BLOCK
    return x.reshape(bsz // bs, bs, n_t, D).transpose(0, 2, 1, 3).reshape(bsz // bs, n_t * bs, D)


def _from_time_major_blocks(x, n_t):
    n_blk = x.shape[0]
    bs = SAMPLE_BLOCK
    return x.reshape(n_blk, n_t, bs, D).transpose(0, 2, 1, 3).reshape(n_blk * bs, n_t, D)


def kernel(x_prompt, x_sample, state_conv_a, state_conv_b, state_lru, state_conv_c, norm_pre, norm_post, w_in, b_in, conv_a_w, conv_a_b, ln_a_g, ln_a_b, w_a_out, conv_b_w, conv_b_b, rg_w_a, rg_b_a, rg_w_x, rg_b_x, rg_lambda, w_b_out, conv_c_w, w_c_out, w_o):
    bf16 = jnp.bfloat16
    depth = w_in.shape[0]
    n_t = x_sample.shape[1]

    rows = [norm_pre, norm_post, conv_a_b, ln_a_g, ln_a_b, conv_b_b, rg_b_a, rg_b_x, rg_lambda]
    vec = jnp.concatenate(
        [v[:, None, :] for v in rows]
        + [b_in.reshape(depth, N_COL_GROUPS, D), conv_a_w, conv_b_w, conv_c_w], axis=1)
    vec = jnp.pad(vec, ((0, 0), (0, NV - vec.shape[1]), (0, 0)))
    weights = (vec, w_in.astype(bf16), w_a_out.astype(bf16), w_b_out.astype(bf16),
               w_c_out.astype(bf16), w_o.astype(bf16),
               _block_diag_groups(rg_w_a).astype(bf16), _block_diag_groups(rg_w_x).astype(bf16))

    xs = _to_time_major_blocks(x_sample)
    sa_tm = state_conv_a.transpose(0, 2, 1, 3)
    sb_tm = state_conv_b.transpose(0, 2, 1, 3)
    sc_tm = state_conv_c.transpose(0, 2, 1, 3)

    xp = x_prompt
    pa, pb, ph, pc = [], [], [], []
    sa, sb, sh, sc = [], [], [], []
    for l in range(depth):
        xp, na, nb, nh, nc = _prompt_layer(xp, weights, l)
        pa.append(na); pb.append(nb); ph.append(nh[:, 0, :]); pc.append(nc)

        xs, ua_new, nb, nh, nc = _sample_layer(xs, sa_tm, sb_tm, state_lru, sc_tm, weights, l)
        ua_new = _from_time_major_blocks(ua_new, n_t)
        sa.append(jnp.concatenate([state_conv_a[l], ua_new], axis=1)[:, -(KA - 1):])
        sb.append(_from_time_major_blocks(nb, KB - 1))
        sh.append(nh)
        sc.append(_from_time_major_blocks(nc, KC - 1))

    return (xp, _from_time_major_blocks(xs, n_t),
            jnp.stack(pa), jnp.stack(pb), jnp.stack(ph), jnp.stack(pc),
            jnp.stack(sa), jnp.stack(sb), jnp.stack(sh), jnp.stack(sc))
```

```python
import functools

import jax
import jax.numpy as jnp
from jax import lax
from jax.experimental import pallas as pl
from jax.experimental.pallas import tpu as pltpu

D = 1024
DEPTH = 4
EPS = 1e-6
LRU_C = 8.0
KA, KB, KC = 31, 4, 3
N_RG_GROUPS = 4
RG_W = D // N_RG_GROUPS

SUBLANES = 8
LANES = 128
N_LANE_TILES = D // LANES

C_AV, C_AG, C_AZ, C_BX, C_BZ, C_CB, C_CC, C_CX, C_CZ, C_G0, C_G1, C_G2 = range(12)
N_COL_GROUPS = 12

V_NORM_PRE, V_NORM_POST, V_CONV_A_B, V_LN_G, V_LN_B, V_CONV_B_B, V_RG_B_A, V_RG_B_X, V_LAMBDA = range(9)
V_B_IN = 9
V_CONV_A_W = V_B_IN + N_COL_GROUPS
V_CONV_B_W = V_CONV_A_W + KA
V_CONV_C_W = V_CONV_B_W + KB
NV = 64
assert V_CONV_C_W + KC <= NV

TM = 256
HIST_A = 32
HIST_S = 8
RC = 16
N_ITER = 1
CONV_A_ROWS = 64
N_SLOTS = 10
SAMPLE_BLOCK = 32
VMEM_LIMIT_BYTES = 60 * 1024 * 1024


def _sigmoid(x):
    return jax.nn.sigmoid(x)


def _silu(x):
    return x * jax.nn.sigmoid(x)


def _softplus(x):
    return jnp.maximum(x, 0.0) + jnp.log1p(jnp.exp(-jnp.abs(x)))


def _lru_coeffs(rgate, igate, softplus_neg_lam, xb):
    log_a = (-LRU_C * rgate) * softplus_neg_lam
    a = jnp.exp(log_a)
    mult = jnp.sqrt(-jnp.tanh(log_a) * (1.0 + a * a))
    return a, mult * (igate * xb)


def _layer_norm(v, gain, bias):
    mu = jnp.mean(v, axis=-1, keepdims=True)
    d = v - mu
    var = jnp.mean(d * d, axis=-1, keepdims=True)
    return (d * lax.rsqrt(var + EPS)) * gain + bias


def _rms_scale(v):
    return v * lax.rsqrt(jnp.mean(v * v, axis=-1, keepdims=True) + EPS)


def _for_chunks(n_rows, rc, body):
    def wrapped(i, carry):
        body(pl.multiple_of(i * rc, rc))
        return carry
    lax.fori_loop(0, n_rows // rc, wrapped, 0, unroll=4)


def _cols(g, n=1):
    return slice(g * D, (g + n) * D)


def _expand_conv_a_weights(vec_ref, wexp_ref):
    for k in range(KA):
        wexp_ref[k * SUBLANES:(k + 1) * SUBLANES, :] = jnp.broadcast_to(
            vec_ref[V_CONV_A_W + k:V_CONV_A_W + k + 1, :], (SUBLANES, D))


S_AV, S_AG, S_AZ, S_BX, S_BZ, S_CB, S_CC, S_CX, S_CZ, S_G0 = range(N_SLOTS)
S_CONV_A = S_R = S_H = S_OUT = S_AV
S_YA = S_G1 = S_AG
S_I = S_MERGED_AB = S_AZ
S_XB = S_YB = S_BX
S_A = S_CONV_C = S_CC
S_U = S_YC = S_CX
S_G2 = S_G0


def _fused_loop(e_body, dots=(), extra=None):
    width = D // N_ITER

    def trip(i):
        off = i * width if isinstance(i, int) else pl.multiple_of(i * width, width)
        for lhs_ref, w_ref, w_col0, dst_ref in dots:
            dst_ref[:, pl.ds(off, width)] = jnp.dot(
                lhs_ref[...], w_ref[:, pl.ds(w_col0 + off, width)], preferred_element_type=jnp.float32)
        if extra is not None:
            extra(i)
        e_body(i)

    if N_ITER == 1:
        trip(0)
    else:
        def body(i, carry):
            trip(i)
            return carry
        lax.fori_loop(0, N_ITER, body, 0)


def _row_trips(*row_bodies):
    rows_per_iter = TM // N_ITER

    def e_body(i):
        for k in range(rows_per_iter // RC):
            r0 = i * rows_per_iter + k * RC
            if not isinstance(r0, int):
                r0 = pl.multiple_of(r0, RC)
            for body in row_bodies:
                body(r0)
    return e_body


def _lane_trips(lane_body):
    tiles_per_iter = N_LANE_TILES // N_ITER

    def e_body(i):
        for lt in range(tiles_per_iter):
            lane = (i * tiles_per_iter + lt) * LANES
            if not isinstance(lane, int):
                lane = pl.multiple_of(lane, LANES)
            lane_body(pl.ds(lane, LANES))
    return e_body


def _shift_in(sub, cur, prev, s):
    return pltpu.roll(jnp.where(sub < SUBLANES - s, cur, prev), s, 0)


def _conv_rows_wide(src_ref, row0, n_blocks, lanes, k_width, w_tile, emit):
    nq = -(-k_width // SUBLANES)
    sub = lax.broadcasted_iota(jnp.int32, (SUBLANES, LANES), 0)
    blocks = {b: src_ref[row0 + SUBLANES * b:row0 + SUBLANES * (b + 1), lanes]
              for b in range(-nq, n_blocks)}
    acc = [None] * n_blocks
    for s in range(SUBLANES):
        taps = [(q, w_tile(SUBLANES * q + s)) for q in range(nq) if SUBLANES * q + s < k_width]

        def qsum(m):
            total = None
            for q, w in taps:
                term = w * blocks[m - q]
                total = term if total is None else total + term
            return total

        if s == 0:
            for m in range(n_blocks):
                acc[m] = qsum(m)
        else:
            prev = qsum(-1)
            for m in range(n_blocks):
                cur = qsum(m)
                acc[m] = acc[m] + _shift_in(sub, cur, prev, s)
                prev = cur
    for m in range(n_blocks):
        emit(m, acc[m])


def _conv_rows_narrow(src_ref, row0, n_blocks, lanes, k_width, w_tiles, emit):
    sub = lax.broadcasted_iota(jnp.int32, (SUBLANES, LANES), 0)
    before = src_ref[row0 - SUBLANES:row0, lanes]
    prev = [w_tiles[s] * before for s in range(1, k_width)]
    for m in range(n_blocks):
        um = src_ref[row0 + SUBLANES * m:row0 + SUBLANES * (m + 1), lanes]
        out = w_tiles[0] * um
        cur = []
        for s in range(1, k_width):
            term = w_tiles[s] * um
            out = out + _shift_in(sub, term, prev[s - 1], s)
            cur.append(term)
        prev = cur
        emit(m, out)


def _prompt_kernel(x_ref, vec_ref, w_in_ref, w_a_ref, w_b_ref, w_c_ref, w_o_ref, w_rga_ref, w_rgx_ref,
                   y_ref, na_ref, nb_ref, nh_ref, nc_ref,
                   hb_ref, z0_ref, z1_ref, yacc_ref, wexp_ref, ua_ref, bx_ref, cc_ref, h_ref, *slot):
    f32, bf16 = jnp.float32, jnp.bfloat16
    t = pl.program_id(1)
    n_blocks = TM // SUBLANES
    assert len(slot) == N_SLOTS

    def row(i):
        return vec_ref[i:i + 1, :]

    def bias(g):
        return row(V_B_IN + g)

    def in_proj(group, dst_slot):
        return (hb_ref, w_in_ref, group * D, slot[dst_slot])

    @pl.when(t == 0)
    def _():
        ua_ref[0:HIST_A, :] = jnp.zeros((HIST_A, D), f32)
        bx_ref[0:HIST_S, :] = jnp.zeros((HIST_S, D), f32)
        cc_ref[0:HIST_S, :] = jnp.zeros((HIST_S, D), f32)
        h_ref[...] = jnp.zeros((SUBLANES, D), f32)
        _expand_conv_a_weights(vec_ref, wexp_ref)

    g_pre = row(V_NORM_PRE)

    def norm_rows(r0):
        rows = pl.ds(r0, RC)
        hb_ref[rows, :] = (_rms_scale(x_ref[rows, :]) * g_pre).astype(bf16)
    _fused_loop(_row_trips(norm_rows))

    for g, s in ((C_AV, S_AV), (C_AG, S_AG)):
        slot[s][...] = jnp.dot(hb_ref[...], w_in_ref[:, _cols(g)], preferred_element_type=f32)
    b_av, b_ag = bias(C_AV), bias(C_AG)

    def glu_rows(r0):
        rows = pl.ds(r0, RC)
        ua_ref[pl.ds(HIST_A + r0, RC), :] = (slot[S_AV][rows, :] + b_av) * _sigmoid(slot[S_AG][rows, :] + b_ag)
    _fused_loop(_row_trips(glu_rows), [in_proj(C_AZ, S_AZ), in_proj(C_BX, S_BX)])

    def conv_a_lanes(lanes):
        def w_tile(j):
            k = KA - 1 - j
            return wexp_ref[k * SUBLANES:(k + 1) * SUBLANES, lanes]

        for row0 in range(0, TM, CONV_A_ROWS):
            def emit(m, out):
                slot[S_CONV_A][row0 + SUBLANES * m:row0 + SUBLANES * (m + 1), lanes] = out
            _conv_rows_wide(ua_ref, HIST_A + row0, CONV_A_ROWS // SUBLANES, lanes, KA, w_tile, emit)
    _fused_loop(_lane_trips(conv_a_lanes),
                [in_proj(C_BZ, S_BZ), in_proj(C_CB, S_CB), in_proj(C_CC, S_CC), in_proj(C_CX, S_CX)])

    conv_a_b, ln_g, ln_b, b_az = row(V_CONV_A_B), row(V_LN_G), row(V_LN_B), bias(C_AZ)
    b_bx, b_cc, b_cx = bias(C_BX), bias(C_CC), bias(C_CX)

    def ln_rows(r0):
        rows = pl.ds(r0, RC)
        y = _layer_norm(slot[S_CONV_A][rows, :] + conv_a_b, ln_g, ln_b)
        z0_ref[rows, :] = (_silu(y) * _silu(slot[S_AZ][rows, :] + b_az)).astype(bf16)

    def conv_in_rows(r0):
        rows = pl.ds(r0, RC)
        bx_ref[pl.ds(HIST_S + r0, RC), :] = slot[S_BX][rows, :] + b_bx
        cc_ref[pl.ds(HIST_S + r0, RC), :] = (slot[S_CC][rows, :] + b_cc) * (slot[S_CX][rows, :] + b_cx)
    _fused_loop(_row_trips(ln_rows, conv_in_rows), [in_proj(C_CZ, S_CZ), in_proj(C_G0, S_G0)])

    def small_conv_lanes(src_ref, v_row, k_width, bias_row, emit_pair):
        def lane_body(lanes):
            w_tiles = [jnp.broadcast_to(vec_ref[v_row + k_width - 1 - j:v_row + k_width - j, lanes],
                                        (SUBLANES, LANES)) for j in range(k_width)]
            bias_tile = None if bias_row is None else vec_ref[bias_row:bias_row + 1, lanes]
            held = []

            def emit(m, out):
                if bias_tile is not None:
                    out = out + bias_tile
                held.append(out)
                if len(held) == 2:
                    emit_pair(slice(SUBLANES * (m - 1), SUBLANES * (m + 1)), lanes,
                              jnp.concatenate(held, axis=0))
                    held.clear()
            _conv_rows_narrow(src_ref, HIST_S, n_blocks, lanes, k_width, w_tiles, emit)
        return lane_body

    def emit_xb(rows, lanes, pair):
        slot[S_XB][rows, lanes] = pair
        z1_ref[rows, lanes] = pair.astype(bf16)
    _fused_loop(_lane_trips(small_conv_lanes(bx_ref, V_CONV_B_W, KB, V_CONV_B_B, emit_xb)),
                [(z0_ref, w_a_ref, 0, slot[S_YA])])

    b_g0 = bias(C_G0)

    def merge_a_rows(r0):
        rows = pl.ds(r0, RC)
        yacc_ref[rows, :] = _sigmoid(slot[S_G0][rows, :] + b_g0) * slot[S_YA][rows, :]

    def rg_dots(i):
        per_iter = N_RG_GROUPS // N_ITER
        for k in range(per_iter):
            g = i * per_iter + k
            lane0 = g * RG_W if isinstance(g, int) else pl.multiple_of(g * RG_W, RG_W)
            lanes = pl.ds(lane0, RG_W)
            xg = z1_ref[:, lanes]
            slot[S_R][:, lanes] = jnp.dot(xg, w_rga_ref[g], preferred_element_type=f32)
            slot[S_I][:, lanes] = jnp.dot(xg, w_rgx_ref[g], preferred_element_type=f32)
    _fused_loop(_row_trips(merge_a_rows), extra=rg_dots)

    softplus_neg_lam = _softplus(-row(V_LAMBDA))
    rg_b_a, rg_b_x = row(V_RG_B_A), row(V_RG_B_X)

    def rg_rows(r0):
        rows = pl.ds(r0, RC)
        a, u = _lru_coeffs(_sigmoid(slot[S_R][rows, :] + rg_b_a), _sigmoid(slot[S_I][rows, :] + rg_b_x),
                           softplus_neg_lam, slot[S_XB][rows, :])
        slot[S_A][rows, :] = a
        slot[S_U][rows, :] = u
    _fused_loop(_row_trips(rg_rows), [in_proj(C_G1, S_G1), in_proj(C_G2, S_G2)])

    def scan_lanes(lanes):
        sub = lax.broadcasted_iota(jnp.int32, (SUBLANES, LANES), 0)
        h = h_ref[:, lanes]
        for m in range(n_blocks):
            rows = slice(SUBLANES * m, SUBLANES * (m + 1))
            a = slot[S_A][rows, lanes]
            u = slot[S_U][rows, lanes]
            for s in (1, 2, 4):
                keep = sub >= s
                a_sh = pltpu.roll(a, s, 0)
                u_sh = pltpu.roll(u, s, 0)
                u = jnp.where(keep, a * u_sh + u, u)
                a = jnp.where(keep, a * a_sh, a)
            hs = a * h + u
            slot[S_H][rows, lanes] = hs
            h = jnp.broadcast_to(hs[SUBLANES - 1:SUBLANES, :], (SUBLANES, LANES))
        h_ref[:, lanes] = h
    _fused_loop(_lane_trips(scan_lanes))

    b_bz = bias(C_BZ)

    def bz_rows(r0):
        rows = pl.ds(r0, RC)
        z0_ref[rows, :] = (slot[S_H][rows, :] * _silu(slot[S_BZ][rows, :] + b_bz)).astype(bf16)
    _fused_loop(_row_trips(bz_rows))

    def emit_conv_c(rows, lanes, pair):
        slot[S_CONV_C][rows, lanes] = pair
    _fused_loop(_lane_trips(small_conv_lanes(cc_ref, V_CONV_C_W, KC, None, emit_conv_c)),
                [(z0_ref, w_b_ref, 0, slot[S_YB])])

    b_cb, b_cz = bias(C_CB), bias(C_CZ)

    def cz_rows(r0):
        rows = pl.ds(r0, RC)
        zc = ((slot[S_CB][rows, :] + b_cb) * slot[S_CONV_C][rows, :]) * _silu(slot[S_CZ][rows, :] + b_cz)
        z1_ref[rows, :] = zc.astype(bf16)
    _fused_loop(_row_trips(cz_rows))

    b_g1, b_g2 = bias(C_G1), bias(C_G2)

    def merge_b_rows(r0):
        rows = pl.ds(r0, RC)
        slot[S_MERGED_AB][rows, :] = yacc_ref[rows, :] + _sigmoid(slot[S_G1][rows, :] + b_g1) * slot[S_YB][rows, :]
    _fused_loop(_row_trips(merge_b_rows), [(z1_ref, w_c_ref, 0, slot[S_YC])])

    def merge_c_rows(r0):
        rows = pl.ds(r0, RC)
        merged = slot[S_MERGED_AB][rows, :] + _sigmoid(slot[S_G2][rows, :] + b_g2) * slot[S_YC][rows, :]
        z0_ref[rows, :] = merged.astype(bf16)
    _fused_loop(_row_trips(merge_c_rows))

    slot[S_OUT][...] = jnp.dot(z0_ref[...], w_o_ref[...], preferred_element_type=f32)
    g_post = row(V_NORM_POST)

    def out_rows(r0):
        rows = pl.ds(r0, RC)
        y_ref[rows, :] = x_ref[rows, :] + _rms_scale(slot[S_OUT][rows, :]) * g_post
    _fused_loop(_row_trips(out_rows))

    @pl.when(t == pl.num_programs(1) - 1)
    def _():
        na_ref[...] = ua_ref[HIST_A + TM - (KA - 1):HIST_A + TM, :]
        nb_ref[...] = bx_ref[HIST_S + TM - (KB - 1):HIST_S + TM, :]
        nc_ref[...] = cc_ref[HIST_S + TM - (KC - 1):HIST_S + TM, :]
        nh_ref[...] = h_ref[0:1, :]

    ua_ref[0:HIST_A, :] = ua_ref[TM:TM + HIST_A, :]
    bx_ref[0:HIST_S, :] = bx_ref[TM:TM + HIST_S, :]
    cc_ref[0:HIST_S, :] = cc_ref[TM:TM + HIST_S, :]


def _resident(shape, layer):
    nd = len(shape)
    return pl.BlockSpec((None,) + tuple(shape[1:]), lambda *_: (layer,) + (0,) * (nd - 1),
                        pipeline_mode=pl.Buffered(1))


def _weight_specs(weights, layer):
    return [_resident(w.shape, layer) for w in weights]


def _prompt_layer(x, weights, layer):
    bsz, seq, _ = x.shape
    assert seq % TM == 0 and TM % (N_ITER * RC) == 0 and TM % CONV_A_ROWS == 0
    assert N_LANE_TILES % N_ITER == 0 and N_RG_GROUPS % N_ITER == 0
    f32 = jnp.float32
    out_shape = (
        jax.ShapeDtypeStruct((bsz, seq, D), f32),
        jax.ShapeDtypeStruct((bsz, KA - 1, D), f32),
        jax.ShapeDtypeStruct((bsz, KB - 1, D), f32),
        jax.ShapeDtypeStruct((bsz, 1, D), f32),
        jax.ShapeDtypeStruct((bsz, KC - 1, D), f32),
    )
    per_seq = lambda rows: pl.BlockSpec((None, rows, D), lambda b, t: (b, 0, 0))
    return pl.pallas_call(
        _prompt_kernel,
        grid=(bsz, seq // TM),
        in_specs=[pl.BlockSpec((None, TM, D), lambda b, t: (b, t, 0))] + _weight_specs(weights, layer),
        out_specs=(pl.BlockSpec((None, TM, D), lambda b, t: (b, t, 0)),
                   per_seq(KA - 1), per_seq(KB - 1), per_seq(1), per_seq(KC - 1)),
        out_shape=out_shape,
        scratch_shapes=[
            pltpu.VMEM((TM, D), jnp.bfloat16),
            pltpu.VMEM((TM, D), jnp.bfloat16),
            pltpu.VMEM((TM, D), jnp.bfloat16),
            pltpu.VMEM((TM, D), f32),
            pltpu.VMEM((KA * SUBLANES, D), f32),
            pltpu.VMEM((HIST_A + TM, D), f32),
            pltpu.VMEM((HIST_S + TM, D), f32),
            pltpu.VMEM((HIST_S + TM, D), f32),
            pltpu.VMEM((SUBLANES, D), f32),
        ] + [pltpu.VMEM((TM, D), f32) for _ in range(N_SLOTS)],
        compiler_params=pltpu.CompilerParams(
            dimension_semantics=("arbitrary", "arbitrary"), vmem_limit_bytes=VMEM_LIMIT_BYTES),
        name=f"prompt_layer{layer}",
    )(x, *weights)


class _Refs:
    def __init__(self, vec, w_in, w_a, w_b, w_c, w_o, w_rga, w_rgx, hb, p, q, z, yacc, wexp):
        self.vec, self.w_in, self.w_a, self.w_b, self.w_c, self.w_o = vec, w_in, w_a, w_b, w_c, w_o
        self.w_rga, self.w_rgx = w_rga, w_rgx
        self.hb, self.p, self.q, self.z, self.yacc, self.wexp = hb, p, q, z, yacc, wexp

    def row(self, i):
        return self.vec[i:i + 1, :]

    def bias(self, g):
        return self.vec[V_B_IN + g:V_B_IN + g + 1, :]


def _in_proj(r, n_rows, g, dst_g):
    r.p[0:n_rows, _cols(dst_g)] = jnp.dot(
        r.hb[0:n_rows, :], r.w_in[:, _cols(g)], preferred_element_type=jnp.float32)


def _layer_body(r, n_rows, x_load, y_store, glu_store, conv_a, bx_store, conv_b, scan, cc_store, conv_c):
    f32, bf16 = jnp.float32, jnp.bfloat16

    g_pre = r.row(V_NORM_PRE)

    def norm_body(r0):
        r.hb[pl.ds(r0, RC), :] = (_rms_scale(x_load(r0)) * g_pre).astype(bf16)
    _for_chunks(n_rows, RC, norm_body)

    def merge(first, last):
        def body(r0, bias):
            rows = pl.ds(r0, RC)
            contrib = _sigmoid(r.p[rows, _cols(1)] + bias) * r.p[rows, _cols(0)]
            if not first:
                contrib = r.yacc[rows, :] + contrib
            if last:
                r.z[rows, :] = contrib.astype(bf16)
            else:
                r.yacc[rows, :] = contrib
        return body

    def branch_out(w_ref, gate_g, first=False, last=False):
        r.p[0:n_rows, _cols(0)] = jnp.dot(r.z[0:n_rows, :], w_ref[...], preferred_element_type=f32)
        _in_proj(r, n_rows, gate_g, 1)
        bias = r.bias(gate_g)
        body = merge(first, last)
        _for_chunks(n_rows, RC, lambda r0: body(r0, bias))

    for j, g in enumerate((C_AV, C_AG, C_AZ)):
        _in_proj(r, n_rows, g, j)
    b_av, b_ag, b_az = r.bias(C_AV), r.bias(C_AG), r.bias(C_AZ)

    def glu_body(r0):
        rows = pl.ds(r0, RC)
        glu_store(r0, (r.p[rows, _cols(0)] + b_av) * _sigmoid(r.p[rows, _cols(1)] + b_ag))
    _for_chunks(n_rows, RC, glu_body)

    conv_a()
    conv_a_b, ln_g, ln_b = r.row(V_CONV_A_B), r.row(V_LN_G), r.row(V_LN_B)

    def ln_body(r0):
        rows = pl.ds(r0, RC)
        y = _layer_norm(r.p[rows, _cols(0)] + conv_a_b, ln_g, ln_b)
        r.z[rows, :] = (_silu(y) * _silu(r.p[rows, _cols(2)] + b_az)).astype(bf16)
    _for_chunks(n_rows, RC, ln_body)
    branch_out(r.w_a, C_G0, first=True)

    for j, g in enumerate((C_BX, C_BZ)):
        _in_proj(r, n_rows, g, j)
    b_bx, b_bz = r.bias(C_BX), r.bias(C_BZ)

    def bx_body(r0):
        bx_store(r0, r.p[pl.ds(r0, RC), _cols(0)] + b_bx)
    _for_chunks(n_rows, RC, bx_body)

    conv_b()

    def xb_cast_body(r0):
        rows = pl.ds(r0, RC)
        r.z[rows, :] = r.p[rows, _cols(2)].astype(bf16)
    _for_chunks(n_rows, RC, xb_cast_body)

    for g in range(N_RG_GROUPS):
        lanes = slice(g * RG_W, (g + 1) * RG_W)
        xg = r.z[0:n_rows, lanes]
        r.q[0:n_rows, lanes] = jnp.dot(xg, r.w_rga[g], preferred_element_type=f32)
        r.q[0:n_rows, slice(D + g * RG_W, D + (g + 1) * RG_W)] = jnp.dot(
            xg, r.w_rgx[g], preferred_element_type=f32)

    softplus_neg_lam = _softplus(-r.row(V_LAMBDA))
    rg_b_a, rg_b_x = r.row(V_RG_B_A), r.row(V_RG_B_X)

    def rg_body(r0):
        rows = pl.ds(r0, RC)
        a, u = _lru_coeffs(_sigmoid(r.q[rows, _cols(0)] + rg_b_a), _sigmoid(r.q[rows, _cols(1)] + rg_b_x),
                           softplus_neg_lam, r.p[rows, _cols(2)])
        r.q[rows, _cols(0)] = a
        r.q[rows, _cols(1)] = u
    _for_chunks(n_rows, RC, rg_body)

    scan()

    def bz_body(r0):
        rows = pl.ds(r0, RC)
        r.z[rows, :] = (r.q[rows, _cols(0)] * _silu(r.p[rows, _cols(1)] + b_bz)).astype(bf16)
    _for_chunks(n_rows, RC, bz_body)
    branch_out(r.w_b, C_G1)

    for j, g in enumerate((C_CB, C_CC, C_CX, C_CZ)):
        _in_proj(r, n_rows, g, j)
    b_cb, b_cc, b_cx, b_cz = r.bias(C_CB), r.bias(C_CC), r.bias(C_CX), r.bias(C_CZ)

    def cc_body(r0):
        rows = pl.ds(r0, RC)
        cc_store(r0, (r.p[rows, _cols(1)] + b_cc) * (r.p[rows, _cols(2)] + b_cx))
    _for_chunks(n_rows, RC, cc_body)

    conv_c()

    def cz_body(r0):
        rows = pl.ds(r0, RC)
        zc = ((r.p[rows, _cols(0)] + b_cb) * r.p[rows, _cols(1)]) * _silu(r.p[rows, _cols(3)] + b_cz)
        r.z[rows, :] = zc.astype(bf16)
    _for_chunks(n_rows, RC, cz_body)
    branch_out(r.w_c, C_G2, last=True)

    r.p[0:n_rows, _cols(0)] = jnp.dot(r.z[0:n_rows, :], r.w_o[...], preferred_element_type=f32)
    g_post = r.row(V_NORM_POST)

    def out_body(r0):
        y_store(r0, x_load(r0) + _rms_scale(r.p[pl.ds(r0, RC), _cols(0)]) * g_post)
    _for_chunks(n_rows, RC, out_body)


def _sample_kernel(n_t, x_ref, sa_ref, sb_ref, sh_ref, sc_ref,
                   vec_ref, w_in_ref, w_a_ref, w_b_ref, w_c_ref, w_o_ref, w_rga_ref, w_rgx_ref,
                   y_ref, ua_out_ref, nb_ref, nh_ref, nc_ref,
                   hb_ref, p_ref, q_ref, z_ref, yacc_ref, wexp_ref, ua_ref, bx_ref, cc_ref):
    bs = SAMPLE_BLOCK
    n_rows = n_t * bs
    r = _Refs(vec_ref, w_in_ref, w_a_ref, w_b_ref, w_c_ref, w_o_ref, w_rga_ref, w_rgx_ref,
              hb_ref, p_ref, q_ref, z_ref, yacc_ref, wexp_ref)
    _expand_conv_a_weights(vec_ref, wexp_ref)

    def flat_store(ref):
        def store(r0, val):
            ref[pl.ds(r0, RC), :] = val
        return store

    def slab_conv(state_ref, new_ref, k_width, tap_weight, dst_g, bias_row):
        n_state = k_width - 1

        def run():
            def body(r0):
                rows = pl.ds(r0, SUBLANES)
                for t in range(n_t):
                    acc = None
                    for k in range(k_width):
                        i = t + k
                        if i < n_state:
                            full = state_ref[i, rows, :]
                        else:
                            full = new_ref[pl.ds((i - n_state) * bs + r0, SUBLANES), :]
                        term = tap_weight(k) * full
                        acc = term if acc is None else acc + term
                    if bias_row is not None:
                        acc = acc + r.row(bias_row)
                    p_ref[pl.ds(t * bs + r0, SUBLANES), _cols(dst_g)] = acc
            _for_chunks(bs, SUBLANES, body)
        return run

    def vec_tap(v_row):
        return lambda k: r.row(v_row + k)

    def scan():
        def body(r0):
            rows = pl.ds(r0, SUBLANES)
            h = sh_ref[rows, :]
            for t in range(n_t):
                trows = pl.ds(t * bs + r0, SUBLANES)
                h = q_ref[trows, _cols(0)] * h + q_ref[trows, _cols(1)]
                q_ref[trows, _cols(0)] = h
            nh_ref[rows, :] = h
        _for_chunks(bs, SUBLANES, body)

    def y_store(r0, val):
        y_ref[pl.ds(r0, RC), :] = val

    _layer_body(
        r, n_rows,
        x_load=lambda r0: x_ref[pl.ds(r0, RC), :],
        y_store=y_store,
        glu_store=flat_store(ua_ref),
        conv_a=slab_conv(sa_ref, ua_ref, KA,
                         lambda k: wexp_ref[k * SUBLANES:(k + 1) * SUBLANES, :], 0, None),
        bx_store=flat_store(bx_ref),
        conv_b=slab_conv(sb_ref, bx_ref, KB, vec_tap(V_CONV_B_W), 2, V_CONV_B_B),
        scan=scan,
        cc_store=flat_store(cc_ref),
        conv_c=slab_conv(sc_ref, cc_ref, KC, vec_tap(V_CONV_C_W), 1, None))

    ua_out_ref[...] = ua_ref[...]
    nb_ref[...] = bx_ref[(n_t - (KB - 1)) * bs:n_t * bs, :]
    nc_ref[...] = cc_ref[(n_t - (KC - 1)) * bs:n_t * bs, :]


def _sample_layer(x_tm, sa_tm, sb_tm, sh, sc_tm, weights, layer):
    n_blk, n_rows, _ = x_tm.shape
    bs = SAMPLE_BLOCK
    n_t = n_rows // bs
    bsz = n_blk * bs
    assert n_t >= KB - 1 and n_t >= KC - 1 and bs % RC == 0
    f32 = jnp.float32
    state = lambda k: pl.BlockSpec((None, k, bs, D), lambda i: (layer, 0, i, 0))
    rows_blk = lambda n: pl.BlockSpec((None, n, D), lambda i: (i, 0, 0))
    out_shape = (
        jax.ShapeDtypeStruct((n_blk, n_rows, D), f32),
        jax.ShapeDtypeStruct((n_blk, n_rows, D), f32),
        jax.ShapeDtypeStruct((n_blk, (KB - 1) * bs, D), f32),
        jax.ShapeDtypeStruct((bsz, D), f32),
        jax.ShapeDtypeStruct((n_blk, (KC - 1) * bs, D), f32),
    )
    return pl.pallas_call(
        functools.partial(_sample_kernel, n_t),
        grid=(n_blk,),
        in_specs=[rows_blk(n_rows), state(KA - 1), state(KB - 1),
                  pl.BlockSpec((None, bs, D), lambda i: (layer, i, 0)), state(KC - 1)]
        + _weight_specs(weights, layer),
        out_specs=(rows_blk(n_rows), rows_blk(n_rows), rows_blk((KB - 1) * bs),
                   pl.BlockSpec((bs, D), lambda i: (i, 0)), rows_blk((KC - 1) * bs)),
        out_shape=out_shape,
        scratch_shapes=[
            pltpu.VMEM((n_rows, D), jnp.bfloat16),
            pltpu.VMEM((n_rows, 4 * D), f32),
            pltpu.VMEM((n_rows, 2 * D), f32),
            pltpu.VMEM((n_rows, D), jnp.bfloat16),
            pltpu.VMEM((n_rows, D), f32),
            pltpu.VMEM((KA * SUBLANES, D), f32),
            pltpu.VMEM((n_rows, D), f32),
            pltpu.VMEM((n_rows, D), f32),
            pltpu.VMEM((n_rows, D), f32),
        ],
        compiler_params=pltpu.CompilerParams(
            dimension_semantics=("arbitrary",), vmem_limit_bytes=VMEM_LIMIT_BYTES),
        name=f"sample_layer{layer}",
    )(x_tm, sa_tm, sb_tm, sh, sc_tm, *weights)


def _block_diag_groups(w):
    depth, n_blocks, blk, _ = w.shape
    per = n_blocks // N_RG_GROUPS
    w = w.reshape(depth, N_RG_GROUPS, per, blk, blk)
    eye = jnp.eye(per, dtype=w.dtype)
    dense = jnp.einsum('lgaij,ab->lgaibj', w, eye)
    return dense.reshape(depth, N_RG_GROUPS, per * blk, per * blk)


def _to_time_major_blocks(x):
    bsz, n_t, _ = x.shape
    bs = SAMPLE_BLOCK
    return x.reshape(bsz // bs, bs, n_t, D).transpose(0, 2, 1, 3).reshape(bsz // bs, n_t * bs, D)


def _from_time_major_blocks(x, n_t):
    n_blk = x.shape[0]
    bs = SAMPLE_BLOCK
    return x.reshape(n_blk, n_t, bs, D).transpose(0, 2, 1, 3).reshape(n_blk * bs, n_t, D)


def kernel(x_prompt, x_sample, state_conv_a, state_conv_b, state_lru, state_conv_c, norm_pre, norm_post, w_in, b_in, conv_a_w, conv_a_b, ln_a_g, ln_a_b, w_a_out, conv_b_w, conv_b_b, rg_w_a, rg_b_a, rg_w_x, rg_b_x, rg_lambda, w_b_out, conv_c_w, w_c_out, w_o):
    bf16 = jnp.bfloat16
    depth = w_in.shape[0]
    n_t = x_sample.shape[1]

    rows = [norm_pre, norm_post, conv_a_b, ln_a_g, ln_a_b, conv_b_b, rg_b_a, rg_b_x, rg_lambda]
    vec = jnp.concatenate(
        [v[:, None, :] for v in rows]
        + [b_in.reshape(depth, N_COL_GROUPS, D), conv_a_w, conv_b_w, conv_c_w], axis=1)
    vec = jnp.pad(vec, ((0, 0), (0, NV - vec.shape[1]), (0, 0)))
    weights = (vec, w_in.astype(bf16), w_a_out.astype(bf16), w_b_out.astype(bf16),
               w_c_out.astype(bf16), w_o.astype(bf16),
               _block_diag_groups(rg_w_a).astype(bf16), _block_diag_groups(rg_w_x).astype(bf16))

    xs = _to_time_major_blocks(x_sample)
    sa_tm = state_conv_a.transpose(0, 2, 1, 3)
    sb_tm = state_conv_b.transpose(0, 2, 1, 3)
    sc_tm = state_conv_c.transpose(0, 2, 1, 3)

    xp = x_prompt
    pa, pb, ph, pc = [], [], [], []
    sa, sb, sh, sc = [], [], [], []
    for l in range(depth):
        xp, na, nb, nh, nc = _prompt_layer(xp, weights, l)
        pa.append(na); pb.append(nb); ph.append(nh[:, 0, :]); pc.append(nc)

        xs, ua_new, nb, nh, nc = _sample_layer(xs, sa_tm, sb_tm, state_lru, sc_tm, weights, l)
        ua_new = _from_time_major_blocks(ua_new, n_t)
        sa.append(jnp.concatenate([state_conv_a[l], ua_new], axis=1)[:, -(KA - 1):])
        sb.append(_from_time_major_blocks(nb, KB - 1))
        sh.append(nh)
        sc.append(_from_time_major_blocks(nc, KC - 1))

    return (xp, _from_time_major_blocks(xs, n_t),
            jnp.stack(pa), jnp.stack(pb), jnp.stack(ph), jnp.stack(pc),
            jnp.stack(sa), jnp.stack(sb), jnp.stack(sh), jnp.stack(sc))
```

```python
import functools

import jax
import jax.numpy as jnp
from jax import lax
from jax.experimental import pallas as pl
from jax.experimental.pallas import tpu as pltpu

D = 1024
DEPTH = 4
EPS = 1e-6
LRU_C = 8.0
KA, KB, KC = 31, 4, 3
N_RG_GROUPS = 4
RG_W = D // N_RG_GROUPS

SUBLANES = 8
LANES = 128
N_LANE_TILES = D // LANES

C_AV, C_AG, C_AZ, C_BX, C_BZ, C_CB, C_CC, C_CX, C_CZ, C_G0, C_G1, C_G2 = range(12)
N_COL_GROUPS = 12

V_NORM_PRE, V_NORM_POST, V_CONV_A_B, V_LN_G, V_LN_B, V_CONV_B_B, V_RG_B_A, V_RG_B_X, V_LAMBDA = range(9)
V_B_IN = 9
V_CONV_A_W = V_B_IN + N_COL_GROUPS
V_CONV_B_W = V_CONV_A_W + KA
V_CONV_C_W = V_CONV_B_W + KB
NV = 64
assert V_CONV_C_W + KC <= NV

TM = 256
HIST_A = 32
HIST_S = 8
RC = 16
CONV_A_ROWS = 64
N_SLOTS = 10
SAMPLE_BLOCK = 32
VMEM_LIMIT_BYTES = 60 * 1024 * 1024


def _sigmoid(x):
    return jax.nn.sigmoid(x)


def _silu(x):
    return x * jax.nn.sigmoid(x)


def _softplus(x):
    return jnp.maximum(x, 0.0) + jnp.log1p(jnp.exp(-jnp.abs(x)))


def _lru_coeffs(rgate, igate, softplus_neg_lam, xb):
    log_a = (-LRU_C * rgate) * softplus_neg_lam
    a = jnp.exp(log_a)
    mult = jnp.sqrt(-jnp.tanh(log_a) * (1.0 + a * a))
    return a, mult * (igate * xb)


def _layer_norm(v, gain, bias):
    mu = jnp.mean(v, axis=-1, keepdims=True)
    d = v - mu
    var = jnp.mean(d * d, axis=-1, keepdims=True)
    return (d * lax.rsqrt(var + EPS)) * gain + bias


def _rms_scale(v):
    return v * lax.rsqrt(jnp.mean(v * v, axis=-1, keepdims=True) + EPS)


def _for_chunks(n_rows, rc, body):
    def wrapped(i, carry):
        body(pl.multiple_of(i * rc, rc))
        return carry
    lax.fori_loop(0, n_rows // rc, wrapped, 0, unroll=4)


def _cols(g, n=1):
    return slice(g * D, (g + n) * D)


def _expand_conv_a_weights(vec_ref, wexp_ref):
    for k in range(KA):
        wexp_ref[k * SUBLANES:(k + 1) * SUBLANES, :] = jnp.broadcast_to(
            vec_ref[V_CONV_A_W + k:V_CONV_A_W + k + 1, :], (SUBLANES, D))


S_AV, S_AG, S_AZ, S_BX, S_BZ, S_CB, S_CC, S_CX, S_CZ, S_G0 = range(N_SLOTS)
S_CONV_A = S_R = S_H = S_OUT = S_AV
S_G1 = S_AG
S_I = S_MERGED_AB = S_AZ
S_XB = S_YB = S_BX
S_YA = S_A = S_CONV_C = S_CC
S_U = S_YC = S_CX
S_G2 = S_G0


def _rows(*row_bodies):
    for r0 in range(0, TM, RC):
        for body in row_bodies:
            body(slice(r0, r0 + RC))


def _zero_after(tile):
    bits = pltpu.bitcast(tile, jnp.uint32)
    bits = lax.shift_right_logical(lax.shift_right_logical(bits, jnp.uint32(16)), jnp.uint32(16))
    return pltpu.bitcast(bits, jnp.float32)


def _shift_in(sub, cur, prev, s):
    return pltpu.roll(jnp.where(sub < SUBLANES - s, cur, prev), s, 0)


def _conv_rows_wide(src_ref, row0, n_blocks, lanes, k_width, w_tile, emit, gate=None):
    nq = -(-k_width // SUBLANES)
    sub = lax.broadcasted_iota(jnp.int32, (SUBLANES, LANES), 0)
    blocks = {b: src_ref[row0 + SUBLANES * b:row0 + SUBLANES * (b + 1), lanes]
              for b in range(-nq, n_blocks)}
    if gate is not None:
        blocks = {b: v + gate for b, v in blocks.items()}
    acc = [None] * n_blocks
    for s in range(SUBLANES):
        taps = [(q, w_tile(SUBLANES * q + s)) for q in range(nq) if SUBLANES * q + s < k_width]

        def qsum(m):
            total = None
            for q, w in taps:
                term = w * blocks[m - q]
                total = term if total is None else total + term
            return total

        if s == 0:
            for m in range(n_blocks):
                acc[m] = qsum(m)
        else:
            prev = qsum(-1)
            for m in range(n_blocks):
                cur = qsum(m)
                acc[m] = acc[m] + _shift_in(sub, cur, prev, s)
                prev = cur
    for m in range(n_blocks):
        emit(m, acc[m])


def _conv_rows_narrow(src_ref, row0, n_blocks, lanes, k_width, w_tiles, emit):
    sub = lax.broadcasted_iota(jnp.int32, (SUBLANES, LANES), 0)
    before = src_ref[row0 - SUBLANES:row0, lanes]
    prev = [w_tiles[s] * before for s in range(1, k_width)]
    for m in range(n_blocks):
        um = src_ref[row0 + SUBLANES * m:row0 + SUBLANES * (m + 1), lanes]
        out = w_tiles[0] * um
        cur = []
        for s in range(1, k_width):
            term = w_tiles[s] * um
            out = out + _shift_in(sub, term, prev[s - 1], s)
            cur.append(term)
        prev = cur
        emit(m, out)


def _prompt_kernel(x_ref, vec_ref, w_in_ref, w_a_ref, w_b_ref, w_c_ref, w_o_ref, w_rga_ref, w_rgx_ref,
                   y_ref, na_ref, nb_ref, nh_ref, nc_ref,
                   hb_ref, z0_ref, z1_ref, yacc_ref, wexp_ref, ua_ref, bx_ref, cc_ref, h_ref, *slot):
    f32, bf16 = jnp.float32, jnp.bfloat16
    t = pl.program_id(1)
    n_blocks = TM // SUBLANES
    assert len(slot) == N_SLOTS
    lane_tiles = [slice(c * LANES, (c + 1) * LANES) for c in range(N_LANE_TILES)]

    av_ref, ag_ref = slot[S_AV], slot[S_AG]

    def row(i):
        return vec_ref[i:i + 1, :]

    def bias(g):
        return row(V_B_IN + g)

    def project(lhs_ref, w_ref, w_cols, dst_ref):
        dst_ref[...] = jnp.dot(lhs_ref[...], w_ref[:, w_cols], preferred_element_type=f32)

    def in_proj(group, dst_slot):
        project(hb_ref, w_in_ref, _cols(group), slot[dst_slot])

    @pl.when(t == 0)
    def _():
        ua_ref[0:HIST_A, :] = jnp.zeros((HIST_A, D), f32)
        bx_ref[0:HIST_S, :] = jnp.zeros((HIST_S, D), f32)
        cc_ref[0:HIST_S, :] = jnp.zeros((HIST_S, D), f32)
        h_ref[...] = jnp.zeros((SUBLANES, D), f32)
        _expand_conv_a_weights(vec_ref, wexp_ref)

    g_pre = row(V_NORM_PRE)

    def norm_rows(rows):
        hb_ref[rows, :] = (_rms_scale(x_ref[rows, :]) * g_pre).astype(bf16)
    _rows(norm_rows)

    in_proj(C_AV, S_AV)
    in_proj(C_AG, S_AG)
    in_proj(C_AZ, S_AZ)
    in_proj(C_BX, S_BX)
    b_av, b_ag = bias(C_AV), bias(C_AG)

    def glu_rows(rows):
        ua_ref[HIST_A + rows.start:HIST_A + rows.stop, :] = (
            (av_ref[rows, :] + b_av) * _sigmoid(ag_ref[rows, :] + b_ag))
    _rows(glu_rows)

    def conv_a_lanes(lanes, gate):
        def w_tile(j):
            k = KA - 1 - j
            return wexp_ref[k * SUBLANES:(k + 1) * SUBLANES, lanes]

        for row0 in range(0, TM, CONV_A_ROWS):
            def emit(m, out):
                slot[S_CONV_A][row0 + SUBLANES * m:row0 + SUBLANES * (m + 1), lanes] = out
            _conv_rows_wide(ua_ref, HIST_A + row0, CONV_A_ROWS // SUBLANES, lanes, KA, w_tile, emit, gate)

    hosted = ((C_BZ, S_BZ), (C_CB, S_CB), (C_CC, S_CC), (C_CX, S_CX), (C_CZ, S_CZ), (C_G0, S_G0), (C_G1, S_G1))
    gate = None
    for c, lanes in enumerate(lane_tiles):
        conv_a_lanes(lanes, gate)
        if c < len(hosted):
            g, s = hosted[c]
            in_proj(g, s)
            gate = _zero_after(slot[s][TM - SUBLANES:TM, D - LANES:D])

    conv_a_b, ln_g, ln_b, b_az = row(V_CONV_A_B), row(V_LN_G), row(V_LN_B), bias(C_AZ)
    b_bx, b_cc, b_cx = bias(C_BX), bias(C_CC), bias(C_CX)

    def ln_rows(rows):
        y = _layer_norm(slot[S_CONV_A][rows, :] + conv_a_b, ln_g, ln_b)
        z0_ref[rows, :] = (_silu(y) * _silu(slot[S_AZ][rows, :] + b_az)).astype(bf16)

    def conv_in_rows(rows):
        dst = slice(HIST_S + rows.start, HIST_S + rows.stop)
        bx_ref[dst, :] = slot[S_BX][rows, :] + b_bx
        cc_ref[dst, :] = (slot[S_CC][rows, :] + b_cc) * (slot[S_CX][rows, :] + b_cx)
    _rows(ln_rows, conv_in_rows)

    def small_conv(src_ref, v_row, k_width, bias_row, emit_pair):
        for lanes in lane_tiles:
            w_tiles = [jnp.broadcast_to(vec_ref[v_row + k_width - 1 - j:v_row + k_width - j, lanes],
                                        (SUBLANES, LANES)) for j in range(k_width)]
            bias_tile = None if bias_row is None else vec_ref[bias_row:bias_row + 1, lanes]
            held = []

            def emit(m, out):
                if bias_tile is not None:
                    out = out + bias_tile
                held.append(out)
                if len(held) == 2:
                    emit_pair(slice(SUBLANES * (m - 1), SUBLANES * (m + 1)), lanes,
                              jnp.concatenate(held, axis=0))
                    held.clear()
            _conv_rows_narrow(src_ref, HIST_S, n_blocks, lanes, k_width, w_tiles, emit)

    def emit_xb(rows, lanes, pair):
        slot[S_XB][rows, lanes] = pair
        z1_ref[rows, lanes] = pair.astype(bf16)
    project(z0_ref, w_a_ref, slice(None), slot[S_YA])
    small_conv(bx_ref, V_CONV_B_W, KB, V_CONV_B_B, emit_xb)

    for g in range(N_RG_GROUPS):
        lanes = slice(g * RG_W, (g + 1) * RG_W)
        xg = z1_ref[:, lanes]
        slot[S_R][:, lanes] = jnp.dot(xg, w_rga_ref[g], preferred_element_type=f32)
        slot[S_I][:, lanes] = jnp.dot(xg, w_rgx_ref[g], preferred_element_type=f32)
    b_g0 = bias(C_G0)

    def merge_a_rows(rows):
        yacc_ref[rows, :] = _sigmoid(slot[S_G0][rows, :] + b_g0) * slot[S_YA][rows, :]
    _rows(merge_a_rows)

    in_proj(C_G2, S_G2)
    softplus_neg_lam = _softplus(-row(V_LAMBDA))
    rg_b_a, rg_b_x = row(V_RG_B_A), row(V_RG_B_X)

    def rg_rows(rows):
        a, u = _lru_coeffs(_sigmoid(slot[S_R][rows, :] + rg_b_a), _sigmoid(slot[S_I][rows, :] + rg_b_x),
                           softplus_neg_lam, slot[S_XB][rows, :])
        slot[S_A][rows, :] = a
        slot[S_U][rows, :] = u
    _rows(rg_rows)

    sub = lax.broadcasted_iota(jnp.int32, (SUBLANES, LANES), 0)
    for lanes in lane_tiles:
        h = h_ref[:, lanes]
        for m in range(n_blocks):
            rows = slice(SUBLANES * m, SUBLANES * (m + 1))
            a = slot[S_A][rows, lanes]
            u = slot[S_U][rows, lanes]
            for s in (1, 2, 4):
                keep = sub >= s
                a_sh = pltpu.roll(a, s, 0)
                u_sh = pltpu.roll(u, s, 0)
                u = jnp.where(keep, a * u_sh + u, u)
                a = jnp.where(keep, a * a_sh, a)
            hs = a * h + u
            slot[S_H][rows, lanes] = hs
            h = jnp.broadcast_to(hs[SUBLANES - 1:SUBLANES, :], (SUBLANES, LANES))
        h_ref[:, lanes] = h

    b_bz = bias(C_BZ)

    def bz_rows(rows):
        z0_ref[rows, :] = (slot[S_H][rows, :] * _silu(slot[S_BZ][rows, :] + b_bz)).astype(bf16)
    _rows(bz_rows)

    def emit_conv_c(rows, lanes, pair):
        slot[S_CONV_C][rows, lanes] = pair
    project(z0_ref, w_b_ref, slice(None), slot[S_YB])
    small_conv(cc_ref, V_CONV_C_W, KC, None, emit_conv_c)

    b_cb, b_cz = bias(C_CB), bias(C_CZ)

    def cz_rows(rows):
        zc = ((slot[S_CB][rows, :] + b_cb) * slot[S_CONV_C][rows, :]) * _silu(slot[S_CZ][rows, :] + b_cz)
        z1_ref[rows, :] = zc.astype(bf16)
    _rows(cz_rows)

    b_g1, b_g2 = bias(C_G1), bias(C_G2)
    project(z1_ref, w_c_ref, slice(None), slot[S_YC])

    def merge_b_rows(rows):
        slot[S_MERGED_AB][rows, :] = yacc_ref[rows, :] + _sigmoid(slot[S_G1][rows, :] + b_g1) * slot[S_YB][rows, :]
    _rows(merge_b_rows)

    def merge_c_rows(rows):
        merged = slot[S_MERGED_AB][rows, :] + _sigmoid(slot[S_G2][rows, :] + b_g2) * slot[S_YC][rows, :]
        z0_ref[rows, :] = merged.astype(bf16)
    _rows(merge_c_rows)

    project(z0_ref, w_o_ref, slice(None), slot[S_OUT])
    g_post = row(V_NORM_POST)

    def out_rows(rows):
        y_ref[rows, :] = x_ref[rows, :] + _rms_scale(slot[S_OUT][rows, :]) * g_post
    _rows(out_rows)

    @pl.when(t == pl.num_programs(1) - 1)
    def _():
        na_ref[...] = ua_ref[HIST_A + TM - (KA - 1):HIST_A + TM, :]
        nb_ref[...] = bx_ref[HIST_S + TM - (KB - 1):HIST_S + TM, :]
        nc_ref[...] = cc_ref[HIST_S + TM - (KC - 1):HIST_S + TM, :]
        nh_ref[...] = h_ref[0:1, :]

    ua_ref[0:HIST_A, :] = ua_ref[TM:TM + HIST_A, :]
    bx_ref[0:HIST_S, :] = bx_ref[TM:TM + HIST_S, :]
    cc_ref[0:HIST_S, :] = cc_ref[TM:TM + HIST_S, :]


def _resident(shape, layer):
    nd = len(shape)
    return pl.BlockSpec((None,) + tuple(shape[1:]), lambda *_: (layer,) + (0,) * (nd - 1),
                        pipeline_mode=pl.Buffered(1))


def _weight_specs(weights, layer):
    return [_resident(w.shape, layer) for w in weights]


def _prompt_layer(x, weights, layer):
    bsz, seq, _ = x.shape
    assert seq % TM == 0 and TM % RC == 0 and TM % CONV_A_ROWS == 0
    n_tiles = seq // TM
    f32 = jnp.float32
    out_shape = (
        jax.ShapeDtypeStruct((bsz, seq, D), f32),
        jax.ShapeDtypeStruct((bsz, KA - 1, D), f32),
        jax.ShapeDtypeStruct((bsz, KB - 1, D), f32),
        jax.ShapeDtypeStruct((bsz, 1, D), f32),
        jax.ShapeDtypeStruct((bsz, KC - 1, D), f32),
    )

    per_seq = lambda rows: pl.BlockSpec((None, rows, D), lambda b, t: (b, 0, 0))
    return pl.pallas_call(
        _prompt_kernel,
        grid=(bsz, n_tiles),
        in_specs=[pl.BlockSpec((None, TM, D), lambda b, t: (b, t, 0))] + _weight_specs(weights, layer),
        out_specs=(pl.BlockSpec((None, TM, D), lambda b, t: (b, t, 0)),
                   per_seq(KA - 1), per_seq(KB - 1), per_seq(1), per_seq(KC - 1)),
        out_shape=out_shape,
        scratch_shapes=[
            pltpu.VMEM((TM, D), jnp.bfloat16),
            pltpu.VMEM((TM, D), jnp.bfloat16),
            pltpu.VMEM((TM, D), jnp.bfloat16),
            pltpu.VMEM((TM, D), f32),
            pltpu.VMEM((KA * SUBLANES, D), f32),
            pltpu.VMEM((HIST_A + TM, D), f32),
            pltpu.VMEM((HIST_S + TM, D), f32),
            pltpu.VMEM((HIST_S + TM, D), f32),
            pltpu.VMEM((SUBLANES, D), f32),
        ] + [pltpu.VMEM((TM, D), f32) for _ in range(N_SLOTS)],
        compiler_params=pltpu.CompilerParams(
            dimension_semantics=("arbitrary", "arbitrary"), vmem_limit_bytes=VMEM_LIMIT_BYTES),
        name=f"prompt_layer{layer}",
    )(x, *weights)


class _Refs:
    def __init__(self, vec, w_in, w_a, w_b, w_c, w_o, w_rga, w_rgx, hb, p, q, z, yacc, wexp):
        self.vec, self.w_in, self.w_a, self.w_b, self.w_c, self.w_o = vec, w_in, w_a, w_b, w_c, w_o
        self.w_rga, self.w_rgx = w_rga, w_rgx
        self.hb, self.p, self.q, self.z, self.yacc, self.wexp = hb, p, q, z, yacc, wexp

    def row(self, i):
        return self.vec[i:i + 1, :]

    def bias(self, g):
        return self.vec[V_B_IN + g:V_B_IN + g + 1, :]


def _in_proj(r, n_rows, g, dst_g):
    r.p[0:n_rows, _cols(dst_g)] = jnp.dot(
        r.hb[0:n_rows, :], r.w_in[:, _cols(g)], preferred_element_type=jnp.float32)


def _layer_body(r, n_rows, x_load, y_store, glu_store, conv_a, bx_store, conv_b, scan, cc_store, conv_c):
    f32, bf16 = jnp.float32, jnp.bfloat16

    g_pre = r.row(V_NORM_PRE)

    def norm_body(r0):
        r.hb[pl.ds(r0, RC), :] = (_rms_scale(x_load(r0)) * g_pre).astype(bf16)
    _for_chunks(n_rows, RC, norm_body)

    def merge(first, last):
        def body(r0, bias):
            rows = pl.ds(r0, RC)
            contrib = _sigmoid(r.p[rows, _cols(1)] + bias) * r.p[rows, _cols(0)]
            if not first:
                contrib = r.yacc[rows, :] + contrib
            if last:
                r.z[rows, :] = contrib.astype(bf16)
            else:
                r.yacc[rows, :] = contrib
        return body

    def branch_out(w_ref, gate_g, first=False, last=False):
        r.p[0:n_rows, _cols(0)] = jnp.dot(r.z[0:n_rows, :], w_ref[...], preferred_element_type=f32)
        _in_proj(r, n_rows, gate_g, 1)
        bias = r.bias(gate_g)
        body = merge(first, last)
        _for_chunks(n_rows, RC, lambda r0: body(r0, bias))

    for j, g in enumerate((C_AV, C_AG, C_AZ)):
        _in_proj(r, n_rows, g, j)
    b_av, b_ag, b_az = r.bias(C_AV), r.bias(C_AG), r.bias(C_AZ)

    def glu_body(r0):
        rows = pl.ds(r0, RC)
        glu_store(r0, (r.p[rows, _cols(0)] + b_av) * _sigmoid(r.p[rows, _cols(1)] + b_ag))
    _for_chunks(n_rows, RC, glu_body)

    conv_a()
    conv_a_b, ln_g, ln_b = r.row(V_CONV_A_B), r.row(V_LN_G), r.row(V_LN_B)

    def ln_body(r0):
        rows = pl.ds(r0, RC)
        y = _layer_norm(r.p[rows, _cols(0)] + conv_a_b, ln_g, ln_b)
        r.z[rows, :] = (_silu(y) * _silu(r.p[rows, _cols(2)] + b_az)).astype(bf16)
    _for_chunks(n_rows, RC, ln_body)
    branch_out(r.w_a, C_G0, first=True)

    for j, g in enumerate((C_BX, C_BZ)):
        _in_proj(r, n_rows, g, j)
    b_bx, b_bz = r.bias(C_BX), r.bias(C_BZ)

    def bx_body(r0):
        bx_store(r0, r.p[pl.ds(r0, RC), _cols(0)] + b_bx)
    _for_chunks(n_rows, RC, bx_body)

    conv_b()

    def xb_cast_body(r0):
        rows = pl.ds(r0, RC)
        r.z[rows, :] = r.p[rows, _cols(2)].astype(bf16)
    _for_chunks(n_rows, RC, xb_cast_body)

    for g in range(N_RG_GROUPS):
        lanes = slice(g * RG_W, (g + 1) * RG_W)
        xg = r.z[0:n_rows, lanes]
        r.q[0:n_rows, lanes] = jnp.dot(xg, r.w_rga[g], preferred_element_type=f32)
        r.q[0:n_rows, slice(D + g * RG_W, D + (g + 1) * RG_W)] = jnp.dot(
            xg, r.w_rgx[g], preferred_element_type=f32)

    softplus_neg_lam = _softplus(-r.row(V_LAMBDA))
    rg_b_a, rg_b_x = r.row(V_RG_B_A), r.row(V_RG_B_X)

    def rg_body(r0):
        rows = pl.ds(r0, RC)
        a, u = _lru_coeffs(_sigmoid(r.q[rows, _cols(0)] + rg_b_a), _sigmoid(r.q[rows, _cols(1)] + rg_b_x),
                           softplus_neg_lam, r.p[rows, _cols(2)])
        r.q[rows, _cols(0)] = a
        r.q[rows, _cols(1)] = u
    _for_chunks(n_rows, RC, rg_body)

    scan()

    def bz_body(r0):
        rows = pl.ds(r0, RC)
        r.z[rows, :] = (r.q[rows, _cols(0)] * _silu(r.p[rows, _cols(1)] + b_bz)).astype(bf16)
    _for_chunks(n_rows, RC, bz_body)
    branch_out(r.w_b, C_G1)

    for j, g in enumerate((C_CB, C_CC, C_CX, C_CZ)):
        _in_proj(r, n_rows, g, j)
    b_cb, b_cc, b_cx, b_cz = r.bias(C_CB), r.bias(C_CC), r.bias(C_CX), r.bias(C_CZ)

    def cc_body(r0):
        rows = pl.ds(r0, RC)
        cc_store(r0, (r.p[rows, _cols(1)] + b_cc) * (r.p[rows, _cols(2)] + b_cx))
    _for_chunks(n_rows, RC, cc_body)

    conv_c()

    def cz_body(r0):
        rows = pl.ds(r0, RC)
        zc = ((r.p[rows, _cols(0)] + b_cb) * r.p[rows, _cols(1)]) * _silu(r.p[rows, _cols(3)] + b_cz)
        r.z[rows, :] = zc.astype(bf16)
    _for_chunks(n_rows, RC, cz_body)
    branch_out(r.w_c, C_G2, last=True)

    r.p[0:n_rows, _cols(0)] = jnp.dot(r.z[0:n_rows, :], r.w_o[...], preferred_element_type=f32)
    g_post = r.row(V_NORM_POST)

    def out_body(r0):
        y_store(r0, x_load(r0) + _rms_scale(r.p[pl.ds(r0, RC), _cols(0)]) * g_post)
    _for_chunks(n_rows, RC, out_body)


def _sample_kernel(n_t, x_ref, sa_ref, sb_ref, sh_ref, sc_ref,
                   vec_ref, w_in_ref, w_a_ref, w_b_ref, w_c_ref, w_o_ref, w_rga_ref, w_rgx_ref,
                   y_ref, ua_out_ref, nb_ref, nh_ref, nc_ref,
                   hb_ref, p_ref, q_ref, z_ref, yacc_ref, wexp_ref, ua_ref, bx_ref, cc_ref):
    bs = SAMPLE_BLOCK
    n_rows = n_t * bs
    r = _Refs(vec_ref, w_in_ref, w_a_ref, w_b_ref, w_c_ref, w_o_ref, w_rga_ref, w_rgx_ref,
              hb_ref, p_ref, q_ref, z_ref, yacc_ref, wexp_ref)
    _expand_conv_a_weights(vec_ref, wexp_ref)

    def flat_store(ref):
        def store(r0, val):
            ref[pl.ds(r0, RC), :] = val
        return store

    def slab_conv(state_ref, new_ref, k_width, tap_weight, dst_g, bias_row):
        n_state = k_width - 1

        def run():
            def body(r0):
                rows = pl.ds(r0, SUBLANES)
                for t in range(n_t):
                    acc = None
                    for k in range(k_width):
                        i = t + k
                        if i < n_state:
                            full = state_ref[i, rows, :]
                        else:
                            full = new_ref[pl.ds((i - n_state) * bs + r0, SUBLANES), :]
                        term = tap_weight(k) * full
                        acc = term if acc is None else acc + term
                    if bias_row is not None:
                        acc = acc + r.row(bias_row)
                    p_ref[pl.ds(t * bs + r0, SUBLANES), _cols(dst_g)] = acc
            _for_chunks(bs, SUBLANES, body)
        return run

    def vec_tap(v_row):
        return lambda k: r.row(v_row + k)

    def scan():
        def body(r0):
            rows = pl.ds(r0, SUBLANES)
            h = sh_ref[rows, :]
            for t in range(n_t):
                trows = pl.ds(t * bs + r0, SUBLANES)
                h = q_ref[trows, _cols(0)] * h + q_ref[trows, _cols(1)]
                q_ref[trows, _cols(0)] = h
            nh_ref[rows, :] = h
        _for_chunks(bs, SUBLANES, body)

    def y_store(r0, val):
        y_ref[pl.ds(r0, RC), :] = val

    _layer_body(
        r, n_rows,
        x_load=lambda r0: x_ref[pl.ds(r0, RC), :],
        y_store=y_store,
        glu_store=flat_store(ua_ref),
        conv_a=slab_conv(sa_ref, ua_ref, KA,
                         lambda k: wexp_ref[k * SUBLANES:(k + 1) * SUBLANES, :], 0, None),
        bx_store=flat_store(bx_ref),
        conv_b=slab_conv(sb_ref, bx_ref, KB, vec_tap(V_CONV_B_W), 2, V_CONV_B_B),
        scan=scan,
        cc_store=flat_store(cc_ref),
        conv_c=slab_conv(sc_ref, cc_ref, KC, vec_tap(V_CONV_C_W), 1, None))

    ua_out_ref[...] = ua_ref[...]
    nb_ref[...] = bx_ref[(n_t - (KB - 1)) * bs:n_t * bs, :]
    nc_ref[...] = cc_ref[(n_t - (KC - 1)) * bs:n_t * bs, :]


def _sample_layer(x_tm, sa_tm, sb_tm, sh, sc_tm, weights, layer):
    n_blk, n_rows, _ = x_tm.shape
    bs = SAMPLE_BLOCK
    n_t = n_rows // bs
    bsz = n_blk * bs
    assert n_t >= KB - 1 and n_t >= KC - 1 and bs % RC == 0
    f32 = jnp.float32
    state = lambda k: pl.BlockSpec((None, k, bs, D), lambda i: (layer, 0, i, 0))
    rows_blk = lambda n: pl.BlockSpec((None, n, D), lambda i: (i, 0, 0))
    out_shape = (
        jax.ShapeDtypeStruct((n_blk, n_rows, D), f32),
        jax.ShapeDtypeStruct((n_blk, n_rows, D), f32),
        jax.ShapeDtypeStruct((n_blk, (KB - 1) * bs, D), f32),
        jax.ShapeDtypeStruct((bsz, D), f32),
        jax.ShapeDtypeStruct((n_blk, (KC - 1) * bs, D), f32),
    )
    return pl.pallas_call(
        functools.partial(_sample_kernel, n_t),
        grid=(n_blk,),
        in_specs=[rows_blk(n_rows), state(KA - 1), state(KB - 1),
                  pl.BlockSpec((None, bs, D), lambda i: (layer, i, 0)), state(KC - 1)]
        + _weight_specs(weights, layer),
        out_specs=(rows_blk(n_rows), rows_blk(n_rows), rows_blk((KB - 1) * bs),
                   pl.BlockSpec((bs, D), lambda i: (i, 0)), rows_blk((KC - 1) * bs)),
        out_shape=out_shape,
        scratch_shapes=[
            pltpu.VMEM((n_rows, D), jnp.bfloat16),
            pltpu.VMEM((n_rows, 4 * D), f32),
            pltpu.VMEM((n_rows, 2 * D), f32),
            pltpu.VMEM((n_rows, D), jnp.bfloat16),
            pltpu.VMEM((n_rows, D), f32),
            pltpu.VMEM((KA * SUBLANES, D), f32),
            pltpu.VMEM((n_rows, D), f32),
            pltpu.VMEM((n_rows, D), f32),
            pltpu.VMEM((n_rows, D), f32),
        ],
        compiler_params=pltpu.CompilerParams(
            dimension_semantics=("arbitrary",), vmem_limit_bytes=VMEM_LIMIT_BYTES),
        name=f"sample_layer{layer}",
    )(x_tm, sa_tm, sb_tm, sh, sc_tm, *weights)


def _block_diag_groups(w):
    depth, n_blocks, blk, _ = w.shape
    per = n_blocks // N_RG_GROUPS
    w = w.reshape(depth, N_RG_GROUPS, per, blk, blk)
    eye = jnp.eye(per, dtype=w.dtype)
    dense = jnp.einsum('lgaij,ab->lgaibj', w, eye)
    return dense.reshape(depth, N_RG_GROUPS, per * blk, per * blk)


def _to_time_major_blocks(x):
    bsz, n_t, _ = x.shape
    bs = SAMPLE_BLOCK
    return x.reshape(bsz // bs, bs, n_t, D).transpose(0, 2, 1, 3).reshape(bsz // bs, n_t * bs, D)


def _from_time_major_blocks(x, n_t):
    n_blk = x.shape[0]
    bs = SAMPLE_BLOCK
    return x.reshape(n_blk, n_t, bs, D).transpose(0, 2, 1, 3).reshape(n_blk * bs, n_t, D)


def kernel(x_prompt, x_sample, state_conv_a, state_conv_b, state_lru, state_conv_c, norm_pre, norm_post, w_in, b_in, conv_a_w, conv_a_b, ln_a_g, ln_a_b, w_a_out, conv_b_w, conv_b_b, rg_w_a, rg_b_a, rg_w_x, rg_b_x, rg_lambda, w_b_out, conv_c_w, w_c_out, w_o):
    bf16 = jnp.bfloat16
    depth = w_in.shape[0]
    n_t = x_sample.shape[1]

    rows = [norm_pre, norm_post, conv_a_b, ln_a_g, ln_a_b, conv_b_b, rg_b_a, rg_b_x, rg_lambda]
    vec = jnp.concatenate(
        [v[:, None, :] for v in rows]
        + [b_in.reshape(depth, N_COL_GROUPS, D), conv_a_w, conv_b_w, conv_c_w], axis=1)
    vec = jnp.pad(vec, ((0, 0), (0, NV - vec.shape[1]), (0, 0)))
    weights = (vec, w_in.astype(bf16), w_a_out.astype(bf16), w_b_out.astype(bf16),
               w_c_out.astype(bf16), w_o.astype(bf16),
               _block_diag_groups(rg_w_a).astype(bf16), _block_diag_groups(rg_w_x).astype(bf16))

    xs = _to_time_major_blocks(x_sample)
    sa_tm = state_conv_a.transpose(0, 2, 1, 3)
    sb_tm = state_conv_b.transpose(0, 2, 1, 3)
    sc_tm = state_conv_c.transpose(0, 2, 1, 3)

    xp = x_prompt
    pa, pb, ph, pc = [], [], [], []
    sa, sb, sh, sc = [], [], [], []
    for l in range(depth):
        xp, na, nb, nh, nc = _prompt_layer(xp, weights, l)
        pa.append(na); pb.append(nb); ph.append(nh[:, 0, :]); pc.append(nc)

        xs, ua_new, nb, nh, nc = _sample_layer(xs, sa_tm, sb_tm, state_lru, sc_tm, weights, l)
        ua_new = _from_time_major_blocks(ua_new, n_t)
        sa.append(jnp.concatenate([state_conv_a[l], ua_new], axis=1)[:, -(KA - 1):])
        sb.append(_from_time_major_blocks(nb, KB - 1))
        sh.append(nh)
        sc.append(_from_time_major_blocks(nc, KC - 1))

    return (xp, _from_time_major_blocks(xs, n_t),
            jnp.stack(pa), jnp.stack(pb), jnp.stack(ph), jnp.stack(pc),
            jnp.stack(sa), jnp.stack(sb), jnp.stack(sh), jnp.stack(sc))
```

```python
import functools

import jax
import jax.numpy as jnp
from jax import lax
from jax.experimental import pallas as pl
from jax.experimental.pallas import tpu as pltpu

D = 1024
DEPTH = 4
EPS = 1e-6
LRU_C = 8.0
KA, KB, KC = 31, 4, 3
N_RG_GROUPS = 4
RG_W = D // N_RG_GROUPS

SUBLANES = 8
LANES = 128
N_LANE_TILES = D // LANES

C_AV, C_AG, C_AZ, C_BX, C_BZ, C_CB, C_CC, C_CX, C_CZ, C_G0, C_G1, C_G2 = range(12)
N_COL_GROUPS = 12

V_NORM_PRE, V_NORM_POST, V_CONV_A_B, V_LN_G, V_LN_B, V_CONV_B_B, V_RG_B_A, V_RG_B_X, V_LAMBDA = range(9)
V_B_IN = 9
V_CONV_A_W = V_B_IN + N_COL_GROUPS
V_CONV_B_W = V_CONV_A_W + KA
V_CONV_C_W = V_CONV_B_W + KB
NV = 64
assert V_CONV_C_W + KC <= NV

TM = 256
HIST_A = 32
HIST_S = 8
RC = 16
CONV_A_ROWS = 64
N_SLOTS = 10
SAMPLE_BLOCK = 32
VMEM_LIMIT_BYTES = 60 * 1024 * 1024


def _sigmoid(x):
    return jax.nn.sigmoid(x)


def _silu(x):
    return x * jax.nn.sigmoid(x)


def _softplus(x):
    return jnp.maximum(x, 0.0) + jnp.log1p(jnp.exp(-jnp.abs(x)))


def _lru_coeffs(rgate, igate, softplus_neg_lam, xb):
    log_a = (-LRU_C * rgate) * softplus_neg_lam
    a = jnp.exp(log_a)
    mult = jnp.sqrt(-jnp.tanh(log_a) * (1.0 + a * a))
    return a, mult * (igate * xb)


def _layer_norm(v, gain, bias):
    mu = jnp.mean(v, axis=-1, keepdims=True)
    d = v - mu
    var = jnp.mean(d * d, axis=-1, keepdims=True)
    return (d * lax.rsqrt(var + EPS)) * gain + bias


def _rms_scale(v):
    return v * lax.rsqrt(jnp.mean(v * v, axis=-1, keepdims=True) + EPS)


def _for_chunks(n_rows, rc, body):
    for r0 in range(0, n_rows, rc):
        body(r0)


def _cols(g, n=1):
    return slice(g * D, (g + n) * D)


def _expand_conv_a_weights(vec_ref, wexp_ref):
    for k in range(KA):
        wexp_ref[k * SUBLANES:(k + 1) * SUBLANES, :] = jnp.broadcast_to(
            vec_ref[V_CONV_A_W + k:V_CONV_A_W + k + 1, :], (SUBLANES, D))


S_AV, S_AG, S_AZ, S_BX, S_BZ, S_CB, S_CC, S_CX, S_CZ, S_G0 = range(N_SLOTS)
S_CONV_A = S_R = S_H = S_OUT = S_AV
S_G1 = S_AG
S_I = S_MERGED_AB = S_AZ
S_XB = S_YB = S_BX
S_YA = S_A = S_CONV_C = S_CC
S_U = S_YC = S_CX
S_G2 = S_G0


def _rows(*row_bodies):
    for r0 in range(0, TM, RC):
        for body in row_bodies:
            body(slice(r0, r0 + RC))


def _zero_after(tile):
    bits = pltpu.bitcast(tile, jnp.uint32)
    bits = lax.shift_right_logical(lax.shift_right_logical(bits, jnp.uint32(16)), jnp.uint32(16))
    return pltpu.bitcast(bits, jnp.float32)


def _shift_in(sub, cur, prev, s):
    return pltpu.roll(jnp.where(sub < SUBLANES - s, cur, prev), s, 0)


def _conv_rows_wide(src_ref, row0, n_blocks, lanes, k_width, w_tile, emit, gate=None):
    nq = -(-k_width // SUBLANES)
    sub = lax.broadcasted_iota(jnp.int32, (SUBLANES, LANES), 0)
    blocks = {b: src_ref[row0 + SUBLANES * b:row0 + SUBLANES * (b + 1), lanes]
              for b in range(-nq, n_blocks)}
    if gate is not None:
        blocks = {b: v + gate for b, v in blocks.items()}
    acc = [None] * n_blocks
    for s in range(SUBLANES):
        taps = [(q, w_tile(SUBLANES * q + s)) for q in range(nq) if SUBLANES * q + s < k_width]

        def qsum(m):
            total = None
            for q, w in taps:
                term = w * blocks[m - q]
                total = term if total is None else total + term
            return total

        if s == 0:
            for m in range(n_blocks):
                acc[m] = qsum(m)
        else:
            prev = qsum(-1)
            for m in range(n_blocks):
                cur = qsum(m)
                acc[m] = acc[m] + _shift_in(sub, cur, prev, s)
                prev = cur
    for m in range(n_blocks):
        emit(m, acc[m])


def _conv_rows_narrow(src_ref, row0, n_blocks, lanes, k_width, w_tiles, emit):
    sub = lax.broadcasted_iota(jnp.int32, (SUBLANES, LANES), 0)
    before = src_ref[row0 - SUBLANES:row0, lanes]
    prev = [w_tiles[s] * before for s in range(1, k_width)]
    for m in range(n_blocks):
        um = src_ref[row0 + SUBLANES * m:row0 + SUBLANES * (m + 1), lanes]
        out = w_tiles[0] * um
        cur = []
        for s in range(1, k_width):
            term = w_tiles[s] * um
            out = out + _shift_in(sub, term, prev[s - 1], s)
            cur.append(term)
        prev = cur
        emit(m, out)


def _prompt_kernel(x_ref, vec_ref, w_in_ref, w_a_ref, w_b_ref, w_c_ref, w_o_ref, w_rga_ref, w_rgx_ref,
                   y_ref, na_ref, nb_ref, nh_ref, nc_ref,
                   hb_ref, z0_ref, z1_ref, yacc_ref, wexp_ref, ua_ref, bx_ref, cc_ref, h_ref, *slot):
    f32, bf16 = jnp.float32, jnp.bfloat16
    t = pl.program_id(1)
    n_blocks = TM // SUBLANES
    assert len(slot) == N_SLOTS
    lane_tiles = [slice(c * LANES, (c + 1) * LANES) for c in range(N_LANE_TILES)]

    av_ref, ag_ref = slot[S_AV], slot[S_AG]

    def row(i):
        return vec_ref[i:i + 1, :]

    def bias(g):
        return row(V_B_IN + g)

    def project(lhs_ref, w_ref, w_cols, dst_ref):
        dst_ref[...] = jnp.dot(lhs_ref[...], w_ref[:, w_cols], preferred_element_type=f32)

    def in_proj(group, dst_slot):
        project(hb_ref, w_in_ref, _cols(group), slot[dst_slot])

    @pl.when(t == 0)
    def _():
        ua_ref[0:HIST_A, :] = jnp.zeros((HIST_A, D), f32)
        bx_ref[0:HIST_S, :] = jnp.zeros((HIST_S, D), f32)
        cc_ref[0:HIST_S, :] = jnp.zeros((HIST_S, D), f32)
        h_ref[...] = jnp.zeros((SUBLANES, D), f32)
        _expand_conv_a_weights(vec_ref, wexp_ref)

    g_pre = row(V_NORM_PRE)

    def norm_rows(rows):
        hb_ref[rows, :] = (_rms_scale(x_ref[rows, :]) * g_pre).astype(bf16)
    _rows(norm_rows)

    in_proj(C_AV, S_AV)
    in_proj(C_AG, S_AG)
    in_proj(C_AZ, S_AZ)
    in_proj(C_BX, S_BX)
    b_av, b_ag = bias(C_AV), bias(C_AG)

    def glu_rows(rows):
        ua_ref[HIST_A + rows.start:HIST_A + rows.stop, :] = (
            (av_ref[rows, :] + b_av) * _sigmoid(ag_ref[rows, :] + b_ag))
    _rows(glu_rows)

    def conv_a_lanes(lanes, gate):
        def w_tile(j):
            k = KA - 1 - j
            return wexp_ref[k * SUBLANES:(k + 1) * SUBLANES, lanes]

        for row0 in range(0, TM, CONV_A_ROWS):
            def emit(m, out):
                slot[S_CONV_A][row0 + SUBLANES * m:row0 + SUBLANES * (m + 1), lanes] = out
            _conv_rows_wide(ua_ref, HIST_A + row0, CONV_A_ROWS // SUBLANES, lanes, KA, w_tile, emit, gate)

    hosted = ((C_BZ, S_BZ), (C_CB, S_CB), (C_CC, S_CC), (C_CX, S_CX), (C_CZ, S_CZ), (C_G0, S_G0), (C_G1, S_G1))
    gate = None
    for c, lanes in enumerate(lane_tiles):
        conv_a_lanes(lanes, gate)
        if c < len(hosted):
            g, s = hosted[c]
            in_proj(g, s)
            gate = _zero_after(slot[s][TM - SUBLANES:TM, D - LANES:D])

    conv_a_b, ln_g, ln_b, b_az = row(V_CONV_A_B), row(V_LN_G), row(V_LN_B), bias(C_AZ)
    b_bx, b_cc, b_cx = bias(C_BX), bias(C_CC), bias(C_CX)

    def ln_rows(rows):
        y = _layer_norm(slot[S_CONV_A][rows, :] + conv_a_b, ln_g, ln_b)
        z0_ref[rows, :] = (_silu(y) * _silu(slot[S_AZ][rows, :] + b_az)).astype(bf16)

    def conv_in_rows(rows):
        dst = slice(HIST_S + rows.start, HIST_S + rows.stop)
        bx_ref[dst, :] = slot[S_BX][rows, :] + b_bx
        cc_ref[dst, :] = (slot[S_CC][rows, :] + b_cc) * (slot[S_CX][rows, :] + b_cx)
    _rows(ln_rows, conv_in_rows)

    def small_conv(src_ref, v_row, k_width, bias_row, emit_pair):
        for lanes in lane_tiles:
            w_tiles = [jnp.broadcast_to(vec_ref[v_row + k_width - 1 - j:v_row + k_width - j, lanes],
                                        (SUBLANES, LANES)) for j in range(k_width)]
            bias_tile = None if bias_row is None else vec_ref[bias_row:bias_row + 1, lanes]
            held = []

            def emit(m, out):
                if bias_tile is not None:
                    out = out + bias_tile
                held.append(out)
                if len(held) == 2:
                    emit_pair(slice(SUBLANES * (m - 1), SUBLANES * (m + 1)), lanes,
                              jnp.concatenate(held, axis=0))
                    held.clear()
            _conv_rows_narrow(src_ref, HIST_S, n_blocks, lanes, k_width, w_tiles, emit)

    def emit_xb(rows, lanes, pair):
        slot[S_XB][rows, lanes] = pair
        z1_ref[rows, lanes] = pair.astype(bf16)
    project(z0_ref, w_a_ref, slice(None), slot[S_YA])
    small_conv(bx_ref, V_CONV_B_W, KB, V_CONV_B_B, emit_xb)

    for g in range(N_RG_GROUPS):
        lanes = slice(g * RG_W, (g + 1) * RG_W)
        xg = z1_ref[:, lanes]
        slot[S_R][:, lanes] = jnp.dot(xg, w_rga_ref[g], preferred_element_type=f32)
        slot[S_I][:, lanes] = jnp.dot(xg, w_rgx_ref[g], preferred_element_type=f32)
    b_g0 = bias(C_G0)

    def merge_a_rows(rows):
        yacc_ref[rows, :] = _sigmoid(slot[S_G0][rows, :] + b_g0) * slot[S_YA][rows, :]
    _rows(merge_a_rows)

    in_proj(C_G2, S_G2)
    softplus_neg_lam = _softplus(-row(V_LAMBDA))
    rg_b_a, rg_b_x = row(V_RG_B_A), row(V_RG_B_X)

    def rg_rows(rows):
        a, u = _lru_coeffs(_sigmoid(slot[S_R][rows, :] + rg_b_a), _sigmoid(slot[S_I][rows, :] + rg_b_x),
                           softplus_neg_lam, slot[S_XB][rows, :])
        slot[S_A][rows, :] = a
        slot[S_U][rows, :] = u
    _rows(rg_rows)

    sub = lax.broadcasted_iota(jnp.int32, (SUBLANES, LANES), 0)
    for lanes in lane_tiles:
        h = h_ref[:, lanes]
        for m in range(n_blocks):
            rows = slice(SUBLANES * m, SUBLANES * (m + 1))
            a = slot[S_A][rows, lanes]
            u = slot[S_U][rows, lanes]
            for s in (1, 2, 4):
                keep = sub >= s
                a_sh = pltpu.roll(a, s, 0)
                u_sh = pltpu.roll(u, s, 0)
                u = jnp.where(keep, a * u_sh + u, u)
                a = jnp.where(keep, a * a_sh, a)
            hs = a * h + u
            slot[S_H][rows, lanes] = hs
            h = jnp.broadcast_to(hs[SUBLANES - 1:SUBLANES, :], (SUBLANES, LANES))
        h_ref[:, lanes] = h

    b_bz = bias(C_BZ)

    def bz_rows(rows):
        z0_ref[rows, :] = (slot[S_H][rows, :] * _silu(slot[S_BZ][rows, :] + b_bz)).astype(bf16)
    _rows(bz_rows)

    def emit_conv_c(rows, lanes, pair):
        slot[S_CONV_C][rows, lanes] = pair
    project(z0_ref, w_b_ref, slice(None), slot[S_YB])
    small_conv(cc_ref, V_CONV_C_W, KC, None, emit_conv_c)

    b_cb, b_cz = bias(C_CB), bias(C_CZ)

    def cz_rows(rows):
        zc = ((slot[S_CB][rows, :] + b_cb) * slot[S_CONV_C][rows, :]) * _silu(slot[S_CZ][rows, :] + b_cz)
        z1_ref[rows, :] = zc.astype(bf16)
    _rows(cz_rows)

    b_g1, b_g2 = bias(C_G1), bias(C_G2)
    project(z1_ref, w_c_ref, slice(None), slot[S_YC])

    def merge_b_rows(rows):
        slot[S_MERGED_AB][rows, :] = yacc_ref[rows, :] + _sigmoid(slot[S_G1][rows, :] + b_g1) * slot[S_YB][rows, :]
    _rows(merge_b_rows)

    def merge_c_rows(rows):
        merged = slot[S_MERGED_AB][rows, :] + _sigmoid(slot[S_G2][rows, :] + b_g2) * slot[S_YC][rows, :]
        z0_ref[rows, :] = merged.astype(bf16)
    _rows(merge_c_rows)

    project(z0_ref, w_o_ref, slice(None), slot[S_OUT])
    g_post = row(V_NORM_POST)

    def out_rows(rows):
        y_ref[rows, :] = x_ref[rows, :] + _rms_scale(slot[S_OUT][rows, :]) * g_post
    _rows(out_rows)

    @pl.when(t == pl.num_programs(1) - 1)
    def _():
        na_ref[...] = ua_ref[HIST_A + TM - (KA - 1):HIST_A + TM, :]
        nb_ref[...] = bx_ref[HIST_S + TM - (KB - 1):HIST_S + TM, :]
        nc_ref[...] = cc_ref[HIST_S + TM - (KC - 1):HIST_S + TM, :]
        nh_ref[...] = h_ref[0:1, :]

    ua_ref[0:HIST_A, :] = ua_ref[TM:TM + HIST_A, :]
    bx_ref[0:HIST_S, :] = bx_ref[TM:TM + HIST_S, :]
    cc_ref[0:HIST_S, :] = cc_ref[TM:TM + HIST_S, :]


def _resident(shape, layer):
    nd = len(shape)
    return pl.BlockSpec((None,) + tuple(shape[1:]), lambda *_: (layer,) + (0,) * (nd - 1),
                        pipeline_mode=pl.Buffered(1))


def _weight_specs(weights, layer):
    return [_resident(w.shape, layer) for w in weights]


def _prompt_layer(x, weights, layer):
    bsz, seq, _ = x.shape
    assert seq % TM == 0 and TM % RC == 0 and TM % CONV_A_ROWS == 0
    n_tiles = seq // TM
    f32 = jnp.float32
    out_shape = (
        jax.ShapeDtypeStruct((bsz, seq, D), f32),
        jax.ShapeDtypeStruct((bsz, KA - 1, D), f32),
        jax.ShapeDtypeStruct((bsz, KB - 1, D), f32),
        jax.ShapeDtypeStruct((bsz, 1, D), f32),
        jax.ShapeDtypeStruct((bsz, KC - 1, D), f32),
    )

    per_seq = lambda rows: pl.BlockSpec((None, rows, D), lambda b, t: (b, 0, 0))
    return pl.pallas_call(
        _prompt_kernel,
        grid=(bsz, n_tiles),
        in_specs=[pl.BlockSpec((None, TM, D), lambda b, t: (b, t, 0))] + _weight_specs(weights, layer),
        out_specs=(pl.BlockSpec((None, TM, D), lambda b, t: (b, t, 0)),
                   per_seq(KA - 1), per_seq(KB - 1), per_seq(1), per_seq(KC - 1)),
        out_shape=out_shape,
        scratch_shapes=[
            pltpu.VMEM((TM, D), jnp.bfloat16),
            pltpu.VMEM((TM, D), jnp.bfloat16),
            pltpu.VMEM((TM, D), jnp.bfloat16),
            pltpu.VMEM((TM, D), f32),
            pltpu.VMEM((KA * SUBLANES, D), f32),
            pltpu.VMEM((HIST_A + TM, D), f32),
            pltpu.VMEM((HIST_S + TM, D), f32),
            pltpu.VMEM((HIST_S + TM, D), f32),
            pltpu.VMEM((SUBLANES, D), f32),
        ] + [pltpu.VMEM((TM, D), f32) for _ in range(N_SLOTS)],
        compiler_params=pltpu.CompilerParams(
            dimension_semantics=("arbitrary", "arbitrary"), vmem_limit_bytes=VMEM_LIMIT_BYTES),
        name=f"prompt_layer{layer}",
    )(x, *weights)


class _Refs:
    def __init__(self, vec, w_in, w_a, w_b, w_c, w_o, w_rga, w_rgx, hb, p, q, z, yacc, wexp):
        self.vec, self.w_in, self.w_a, self.w_b, self.w_c, self.w_o = vec, w_in, w_a, w_b, w_c, w_o
        self.w_rga, self.w_rgx = w_rga, w_rgx
        self.hb, self.p, self.q, self.z, self.yacc, self.wexp = hb, p, q, z, yacc, wexp

    def row(self, i):
        return self.vec[i:i + 1, :]

    def bias(self, g):
        return self.vec[V_B_IN + g:V_B_IN + g + 1, :]


def _in_proj(r, n_rows, g, dst_g):
    r.p[0:n_rows, _cols(dst_g)] = jnp.dot(
        r.hb[0:n_rows, :], r.w_in[:, _cols(g)], preferred_element_type=jnp.float32)


def _layer_body(r, n_rows, x_load, y_store, glu_store, conv_a, bx_store, conv_b, scan, cc_store, conv_c):
    f32, bf16 = jnp.float32, jnp.bfloat16

    g_pre = r.row(V_NORM_PRE)

    def norm_body(r0):
        r.hb[pl.ds(r0, RC), :] = (_rms_scale(x_load(r0)) * g_pre).astype(bf16)
    _for_chunks(n_rows, RC, norm_body)

    def merge(first, last):
        def body(r0, bias):
            rows = pl.ds(r0, RC)
            contrib = _sigmoid(r.p[rows, _cols(1)] + bias) * r.p[rows, _cols(0)]
            if not first:
                contrib = r.yacc[rows, :] + contrib
            if last:
                r.z[rows, :] = contrib.astype(bf16)
            else:
                r.yacc[rows, :] = contrib
        return body

    def branch_out(w_ref, gate_g, first=False, last=False):
        r.p[0:n_rows, _cols(0)] = jnp.dot(r.z[0:n_rows, :], w_ref[...], preferred_element_type=f32)
        _in_proj(r, n_rows, gate_g, 1)
        bias = r.bias(gate_g)
        body = merge(first, last)
        _for_chunks(n_rows, RC, lambda r0: body(r0, bias))

    for j, g in enumerate((C_AV, C_AG, C_AZ)):
        _in_proj(r, n_rows, g, j)
    b_av, b_ag, b_az = r.bias(C_AV), r.bias(C_AG), r.bias(C_AZ)

    def glu_body(r0):
        rows = pl.ds(r0, RC)
        glu_store(r0, (r.p[rows, _cols(0)] + b_av) * _sigmoid(r.p[rows, _cols(1)] + b_ag))
    _for_chunks(n_rows, RC, glu_body)

    conv_a()
    conv_a_b, ln_g, ln_b = r.row(V_CONV_A_B), r.row(V_LN_G), r.row(V_LN_B)

    def ln_body(r0):
        rows = pl.ds(r0, RC)
        y = _layer_norm(r.p[rows, _cols(0)] + conv_a_b, ln_g, ln_b)
        r.z[rows, :] = (_silu(y) * _silu(r.p[rows, _cols(2)] + b_az)).astype(bf16)
    _for_chunks(n_rows, RC, ln_body)
    branch_out(r.w_a, C_G0, first=True)

    for j, g in enumerate((C_BX, C_BZ)):
        _in_proj(r, n_rows, g, j)
    b_bx, b_bz = r.bias(C_BX), r.bias(C_BZ)

    def bx_body(r0):
        bx_store(r0, r.p[pl.ds(r0, RC), _cols(0)] + b_bx)
    _for_chunks(n_rows, RC, bx_body)

    conv_b()

    def xb_cast_body(r0):
        rows = pl.ds(r0, RC)
        r.z[rows, :] = r.p[rows, _cols(2)].astype(bf16)
    _for_chunks(n_rows, RC, xb_cast_body)

    for g in range(N_RG_GROUPS):
        lanes = slice(g * RG_W, (g + 1) * RG_W)
        xg = r.z[0:n_rows, lanes]
        r.q[0:n_rows, lanes] = jnp.dot(xg, r.w_rga[g], preferred_element_type=f32)
        r.q[0:n_rows, slice(D + g * RG_W, D + (g + 1) * RG_W)] = jnp.dot(
            xg, r.w_rgx[g], preferred_element_type=f32)

    softplus_neg_lam = _softplus(-r.row(V_LAMBDA))
    rg_b_a, rg_b_x = r.row(V_RG_B_A), r.row(V_RG_B_X)

    def rg_body(r0):
        rows = pl.ds(r0, RC)
        a, u = _lru_coeffs(_sigmoid(r.q[rows, _cols(0)] + rg_b_a), _sigmoid(r.q[rows, _cols(1)] + rg_b_x),
                           softplus_neg_lam, r.p[rows, _cols(2)])
        r.q[rows, _cols(0)] = a
        r.q[rows, _cols(1)] = u
    _for_chunks(n_rows, RC, rg_body)

    scan()

    def bz_body(r0):
        rows = pl.ds(r0, RC)
        r.z[rows, :] = (r.q[rows, _cols(0)] * _silu(r.p[rows, _cols(1)] + b_bz)).astype(bf16)
    _for_chunks(n_rows, RC, bz_body)
    branch_out(r.w_b, C_G1)

    for j, g in enumerate((C_CB, C_CC, C_CX, C_CZ)):
        _in_proj(r, n_rows, g, j)
    b_cb, b_cc, b_cx, b_cz = r.bias(C_CB), r.bias(C_CC), r.bias(C_CX), r.bias(C_CZ)

    def cc_body(r0):
        rows = pl.ds(r0, RC)
        cc_store(r0, (r.p[rows, _cols(1)] + b_cc) * (r.p[rows, _cols(2)] + b_cx))
    _for_chunks(n_rows, RC, cc_body)

    conv_c()

    def cz_body(r0):
        rows = pl.ds(r0, RC)
        zc = ((r.p[rows, _cols(0)] + b_cb) * r.p[rows, _cols(1)]) * _silu(r.p[rows, _cols(3)] + b_cz)
        r.z[rows, :] = zc.astype(bf16)
    _for_chunks(n_rows, RC, cz_body)
    branch_out(r.w_c, C_G2, last=True)

    r.p[0:n_rows, _cols(0)] = jnp.dot(r.z[0:n_rows, :], r.w_o[...], preferred_element_type=f32)
    g_post = r.row(V_NORM_POST)

    def out_body(r0):
        y_store(r0, x_load(r0) + _rms_scale(r.p[pl.ds(r0, RC), _cols(0)]) * g_post)
    _for_chunks(n_rows, RC, out_body)


def _sample_kernel(n_t, x_ref, sa_ref, sb_ref, sh_ref, sc_ref,
                   vec_ref, w_in_ref, w_a_ref, w_b_ref, w_c_ref, w_o_ref, w_rga_ref, w_rgx_ref,
                   y_ref, ua_out_ref, nb_ref, nh_ref, nc_ref,
                   hb_ref, p_ref, q_ref, z_ref, yacc_ref, wexp_ref, ua_ref, bx_ref, cc_ref):
    bs = SAMPLE_BLOCK
    n_rows = n_t * bs
    r = _Refs(vec_ref, w_in_ref, w_a_ref, w_b_ref, w_c_ref, w_o_ref, w_rga_ref, w_rgx_ref,
              hb_ref, p_ref, q_ref, z_ref, yacc_ref, wexp_ref)
    _expand_conv_a_weights(vec_ref, wexp_ref)

    def flat_store(ref):
        def store(r0, val):
            ref[pl.ds(r0, RC), :] = val
        return store

    def slab_conv(state_ref, new_ref, k_width, tap_weight, dst_g, bias_row):
        n_state = k_width - 1

        def run():
            def body(r0):
                rows = pl.ds(r0, SUBLANES)
                for t in range(n_t):
                    acc = None
                    for k in range(k_width):
                        i = t + k
                        if i < n_state:
                            full = state_ref[i, rows, :]
                        else:
                            full = new_ref[pl.ds((i - n_state) * bs + r0, SUBLANES), :]
                        term = tap_weight(k) * full
                        acc = term if acc is None else acc + term
                    if bias_row is not None:
                        acc = acc + r.row(bias_row)
                    p_ref[pl.ds(t * bs + r0, SUBLANES), _cols(dst_g)] = acc
            _for_chunks(bs, SUBLANES, body)
        return run

    def vec_tap(v_row):
        return lambda k: r.row(v_row + k)

    def scan():
        def body(r0):
            rows = pl.ds(r0, SUBLANES)
            h = sh_ref[rows, :]
            for t in range(n_t):
                trows = pl.ds(t * bs + r0, SUBLANES)
                h = q_ref[trows, _cols(0)] * h + q_ref[trows, _cols(1)]
                q_ref[trows, _cols(0)] = h
            nh_ref[rows, :] = h
        _for_chunks(bs, SUBLANES, body)

    def y_store(r0, val):
        y_ref[pl.ds(r0, RC), :] = val

    _layer_body(
        r, n_rows,
        x_load=lambda r0: x_ref[pl.ds(r0, RC), :],
        y_store=y_store,
        glu_store=flat_store(ua_ref),
        conv_a=slab_conv(sa_ref, ua_ref, KA,
                         lambda k: wexp_ref[k * SUBLANES:(k + 1) * SUBLANES, :], 0, None),
        bx_store=flat_store(bx_ref),
        conv_b=slab_conv(sb_ref, bx_ref, KB, vec_tap(V_CONV_B_W), 2, V_CONV_B_B),
        scan=scan,
        cc_store=flat_store(cc_ref),
        conv_c=slab_conv(sc_ref, cc_ref, KC, vec_tap(V_CONV_C_W), 1, None))

    ua_out_ref[...] = ua_ref[...]
    nb_ref[...] = bx_ref[(n_t - (KB - 1)) * bs:n_t * bs, :]
    nc_ref[...] = cc_ref[(n_t - (KC - 1)) * bs:n_t * bs, :]


def _sample_layer(x_tm, sa_tm, sb_tm, sh, sc_tm, weights, layer):
    n_blk, n_rows, _ = x_tm.shape
    bs = SAMPLE_BLOCK
    n_t = n_rows // bs
    bsz = n_blk * bs
    assert n_t >= KB - 1 and n_t >= KC - 1 and bs % RC == 0
    f32 = jnp.float32
    state = lambda k: pl.BlockSpec((None, k, bs, D), lambda i: (layer, 0, i, 0))
    rows_blk = lambda n: pl.BlockSpec((None, n, D), lambda i: (i, 0, 0))
    out_shape = (
        jax.ShapeDtypeStruct((n_blk, n_rows, D), f32),
        jax.ShapeDtypeStruct((n_blk, n_rows, D), f32),
        jax.ShapeDtypeStruct((n_blk, (KB - 1) * bs, D), f32),
        jax.ShapeDtypeStruct((bsz, D), f32),
        jax.ShapeDtypeStruct((n_blk, (KC - 1) * bs, D), f32),
    )
    return pl.pallas_call(
        functools.partial(_sample_kernel, n_t),
        grid=(n_blk,),
        in_specs=[rows_blk(n_rows), state(KA - 1), state(KB - 1),
                  pl.BlockSpec((None, bs, D), lambda i: (layer, i, 0)), state(KC - 1)]
        + _weight_specs(weights, layer),
        out_specs=(rows_blk(n_rows), rows_blk(n_rows), rows_blk((KB - 1) * bs),
                   pl.BlockSpec((bs, D), lambda i: (i, 0)), rows_blk((KC - 1) * bs)),
        out_shape=out_shape,
        scratch_shapes=[
            pltpu.VMEM((n_rows, D), jnp.bfloat16),
            pltpu.VMEM((n_rows, 4 * D), f32),
            pltpu.VMEM((n_rows, 2 * D), f32),
            pltpu.VMEM((n_rows, D), jnp.bfloat16),
            pltpu.VMEM((n_rows, D), f32),
            pltpu.VMEM((KA * SUBLANES, D), f32),
            pltpu.VMEM((n_rows, D), f32),
            pltpu.VMEM((n_rows, D), f32),
            pltpu.VMEM((n_rows, D), f32),
        ],
        compiler_params=pltpu.CompilerParams(
            dimension_semantics=("arbitrary",), vmem_limit_bytes=VMEM_LIMIT_BYTES),
        name=f"sample_layer{layer}",
    )(x_tm, sa_tm, sb_tm, sh, sc_tm, *weights)


def _block_diag_groups(w):
    depth, n_blocks, blk, _ = w.shape
    per = n_blocks // N_RG_GROUPS
    w = w.reshape(depth, N_RG_GROUPS, per, blk, blk)
    eye = jnp.eye(per, dtype=w.dtype)
    dense = jnp.einsum('lgaij,ab->lgaibj', w, eye)
    return dense.reshape(depth, N_RG_GROUPS, per * blk, per * blk)


def _to_time_major_blocks(x):
    bsz, n_t, _ = x.shape
    bs = SAMPLE_BLOCK
    return x.reshape(bsz // bs, bs, n_t, D).transpose(0, 2, 1, 3).reshape(bsz // bs, n_t * bs, D)


def _from_time_major_blocks(x, n_t):
    n_blk = x.shape[0]
    bs = SAMPLE_BLOCK
    return x.reshape(n_blk, n_t, bs, D).transpose(0, 2, 1, 3).reshape(n_blk * bs, n_t, D)


def kernel(x_prompt, x_sample, state_conv_a, state_conv_b, state_lru, state_conv_c, norm_pre, norm_post, w_in, b_in, conv_a_w, conv_a_b, ln_a_g, ln_a_b, w_a_out, conv_b_w, conv_b_b, rg_w_a, rg_b_a, rg_w_x, rg_b_x, rg_lambda, w_b_out, conv_c_w, w_c_out, w_o):
    bf16 = jnp.bfloat16
    depth = w_in.shape[0]
    n_t = x_sample.shape[1]

    rows = [norm_pre, norm_post, conv_a_b, ln_a_g, ln_a_b, conv_b_b, rg_b_a, rg_b_x, rg_lambda]
    vec = jnp.concatenate(
        [v[:, None, :] for v in rows]
        + [b_in.reshape(depth, N_COL_GROUPS, D), conv_a_w, conv_b_w, conv_c_w], axis=1)
    vec = jnp.pad(vec, ((0, 0), (0, NV - vec.shape[1]), (0, 0)))
    weights = (vec, w_in.astype(bf16), w_a_out.astype(bf16), w_b_out.astype(bf16),
               w_c_out.astype(bf16), w_o.astype(bf16),
               _block_diag_groups(rg_w_a).astype(bf16), _block_diag_groups(rg_w_x).astype(bf16))

    xs = _to_time_major_blocks(x_sample)
    sa_tm = state_conv_a.transpose(0, 2, 1, 3)
    sb_tm = state_conv_b.transpose(0, 2, 1, 3)
    sc_tm = state_conv_c.transpose(0, 2, 1, 3)

    xp = x_prompt
    pa, pb, ph, pc = [], [], [], []
    sa, sb, sh, sc = [], [], [], []
    for l in range(depth):
        xp, na, nb, nh, nc = _prompt_layer(xp, weights, l)
        pa.append(na); pb.append(nb); ph.append(nh[:, 0, :]); pc.append(nc)

        xs, ua_new, nb, nh, nc = _sample_layer(xs, sa_tm, sb_tm, state_lru, sc_tm, weights, l)
        ua_new = _from_time_major_blocks(ua_new, n_t)
        sa.append(jnp.concatenate([state_conv_a[l], ua_new], axis=1)[:, -(KA - 1):])
        sb.append(_from_time_major_blocks(nb, KB - 1))
        sh.append(nh)
        sc.append(_from_time_major_blocks(nc, KC - 1))

    return (xp, _from_time_major_blocks(xs, n_t),
            jnp.stack(pa), jnp.stack(pb), jnp.stack(ph), jnp.stack(pc),
            jnp.stack(sa), jnp.stack(sb), jnp.stack(sh), jnp.stack(sc))
```

```python
import functools

import jax
import jax.numpy as jnp
from jax import lax
from jax.experimental import pallas as pl
from jax.experimental.pallas import tpu as pltpu

D = 1024
DEPTH = 4
EPS = 1e-6
LRU_C = 8.0
KA, KB, KC = 31, 4, 3
N_RG_GROUPS = 4
RG_W = D // N_RG_GROUPS

SUBLANES = 8
LANES = 128
N_LANE_TILES = D // LANES

C_AV, C_AG, C_AZ, C_BX, C_BZ, C_CB, C_CC, C_CX, C_CZ, C_G0, C_G1, C_G2 = range(12)
N_COL_GROUPS = 12

V_NORM_PRE, V_NORM_POST, V_CONV_A_B, V_LN_G, V_LN_B, V_CONV_B_B, V_RG_B_A, V_RG_B_X, V_LAMBDA = range(9)
V_B_IN = 9
V_CONV_A_W = V_B_IN + N_COL_GROUPS
V_CONV_B_W = V_CONV_A_W + KA
V_CONV_C_W = V_CONV_B_W + KB
NV = 64
assert V_CONV_C_W + KC <= NV

TM = 256
HIST_A = 32
HIST_S = 8
RC = 16
CONV_A_ROWS = 64
N_SLOTS = 10
SAMPLE_BLOCK = 32
VMEM_LIMIT_BYTES = 60 * 1024 * 1024


def _sigmoid(x):
    return jax.nn.sigmoid(x)


def _silu(x):
    return x * jax.nn.sigmoid(x)


def _softplus(x):
    return jnp.maximum(x, 0.0) + jnp.log1p(jnp.exp(-jnp.abs(x)))


def _lru_coeffs(rgate, igate, softplus_neg_lam, xb):
    log_a = (-LRU_C * rgate) * softplus_neg_lam
    a = jnp.exp(log_a)
    mult = jnp.sqrt(-jnp.tanh(log_a) * (1.0 + a * a))
    return a, mult * (igate * xb)


def _layer_norm(v, gain, bias):
    mu = jnp.mean(v, axis=-1, keepdims=True)
    d = v - mu
    var = jnp.mean(d * d, axis=-1, keepdims=True)
    return (d * lax.rsqrt(var + EPS)) * gain + bias


def _rms_scale(v):
    return v * lax.rsqrt(jnp.mean(v * v, axis=-1, keepdims=True) + EPS)


def _for_chunks(n_rows, rc, body):
    for r0 in range(0, n_rows, rc):
        body(r0)


def _cols(g, n=1):
    return slice(g * D, (g + n) * D)


def _expand_conv_a_weights(vec_ref, wexp_ref):
    for k in range(KA):
        wexp_ref[k * SUBLANES:(k + 1) * SUBLANES, :] = jnp.broadcast_to(
            vec_ref[V_CONV_A_W + k:V_CONV_A_W + k + 1, :], (SUBLANES, D))


S_AV, S_AG, S_AZ, S_BX, S_BZ, S_CB, S_CC, S_CX, S_CZ, S_G0 = range(N_SLOTS)
S_CONV_A = S_R = S_H = S_OUT = S_AV
S_G1 = S_AG
S_I = S_MERGED_AB = S_AZ
S_XB = S_YB = S_BX
S_YA = S_A = S_CONV_C = S_CC
S_U = S_YC = S_CX
S_G2 = S_G0


def _rows(*row_bodies):
    for r0 in range(0, TM, RC):
        for body in row_bodies:
            body(slice(r0, r0 + RC))


def _zero_after(tile):
    bits = pltpu.bitcast(tile, jnp.uint32)
    bits = lax.shift_right_logical(lax.shift_right_logical(bits, jnp.uint32(16)), jnp.uint32(16))
    return pltpu.bitcast(bits, jnp.float32)


def _shift_in(sub, cur, prev, s):
    return pltpu.roll(jnp.where(sub < SUBLANES - s, cur, prev), s, 0)


def _conv_rows_wide(src_ref, row0, n_blocks, lanes, k_width, w_tile, emit, gate=None):
    nq = -(-k_width // SUBLANES)
    sub = lax.broadcasted_iota(jnp.int32, (SUBLANES, LANES), 0)
    blocks = {b: src_ref[row0 + SUBLANES * b:row0 + SUBLANES * (b + 1), lanes]
              for b in range(-nq, n_blocks)}
    if gate is not None:
        blocks = {b: v + gate for b, v in blocks.items()}
    acc = [None] * n_blocks
    for s in range(SUBLANES):
        taps = [(q, w_tile(SUBLANES * q + s)) for q in range(nq) if SUBLANES * q + s < k_width]

        def qsum(m):
            total = None
            for q, w in taps:
                term = w * blocks[m - q]
                total = term if total is None else total + term
            return total

        if s == 0:
            for m in range(n_blocks):
                acc[m] = qsum(m)
        else:
            prev = qsum(-1)
            for m in range(n_blocks):
                cur = qsum(m)
                acc[m] = acc[m] + _shift_in(sub, cur, prev, s)
                prev = cur
    for m in range(n_blocks):
        emit(m, acc[m])


def _conv_rows_narrow(src_ref, row0, n_blocks, lanes, k_width, w_tiles, emit):
    sub = lax.broadcasted_iota(jnp.int32, (SUBLANES, LANES), 0)
    before = src_ref[row0 - SUBLANES:row0, lanes]
    prev = [w_tiles[s] * before for s in range(1, k_width)]
    for m in range(n_blocks):
        um = src_ref[row0 + SUBLANES * m:row0 + SUBLANES * (m + 1), lanes]
        out = w_tiles[0] * um
        cur = []
        for s in range(1, k_width):
            term = w_tiles[s] * um
            out = out + _shift_in(sub, term, prev[s - 1], s)
            cur.append(term)
        prev = cur
        emit(m, out)


def _prompt_kernel(x_ref, vec_ref, w_in_ref, w_a_ref, w_b_ref, w_c_ref, w_o_ref, w_rga_ref, w_rgx_ref,
                   y_ref, na_ref, nb_ref, nh_ref, nc_ref,
                   hb_ref, z0_ref, z1_ref, yacc_ref, wexp_ref, ua_ref, bx_ref, cc_ref, h_ref, *slot):
    f32, bf16 = jnp.float32, jnp.bfloat16
    t = pl.program_id(1)
    n_blocks = TM // SUBLANES
    assert len(slot) == N_SLOTS
    lane_tiles = [slice(c * LANES, (c + 1) * LANES) for c in range(N_LANE_TILES)]

    av_ref, ag_ref = slot[S_AV], slot[S_AG]

    def row(i):
        return vec_ref[i:i + 1, :]

    def bias(g):
        return row(V_B_IN + g)

    def project(lhs_ref, w_ref, w_cols, dst_ref):
        dst_ref[...] = jnp.dot(lhs_ref[...], w_ref[:, w_cols], preferred_element_type=f32)

    def in_proj(group, dst_slot):
        project(hb_ref, w_in_ref, _cols(group), slot[dst_slot])

    @pl.when(t == 0)
    def _():
        ua_ref[0:HIST_A, :] = jnp.zeros((HIST_A, D), f32)
        bx_ref[0:HIST_S, :] = jnp.zeros((HIST_S, D), f32)
        cc_ref[0:HIST_S, :] = jnp.zeros((HIST_S, D), f32)
        h_ref[...] = jnp.zeros((SUBLANES, D), f32)
        _expand_conv_a_weights(vec_ref, wexp_ref)

    g_pre = row(V_NORM_PRE)

    def norm_rows(rows):
        hb_ref[rows, :] = (_rms_scale(x_ref[rows, :]) * g_pre).astype(bf16)
    _rows(norm_rows)

    in_proj(C_AV, S_AV)
    in_proj(C_AG, S_AG)
    b_av, b_ag = bias(C_AV), bias(C_AG)

    def glu_rows(rows):
        ua_ref[HIST_A + rows.start:HIST_A + rows.stop, :] = (
            (av_ref[rows, :] + b_av) * _sigmoid(ag_ref[rows, :] + b_ag))
    _rows(glu_rows)

    def conv_a_lanes(lanes, gate):
        def w_tile(j):
            k = KA - 1 - j
            return wexp_ref[k * SUBLANES:(k + 1) * SUBLANES, lanes]

        for row0 in range(0, TM, CONV_A_ROWS):
            def emit(m, out):
                slot[S_CONV_A][row0 + SUBLANES * m:row0 + SUBLANES * (m + 1), lanes] = out
            _conv_rows_wide(ua_ref, HIST_A + row0, CONV_A_ROWS // SUBLANES, lanes, KA, w_tile, emit, gate)

    hosted = (((C_AZ, S_AZ),), ((C_BX, S_BX),), ((C_CC, S_CC),), ((C_CX, S_CX),), ((C_BZ, S_BZ),),
              ((C_CB, S_CB),), ((C_CZ, S_CZ),), ((C_G0, S_G0), (C_G1, S_G1)))
    gate = None
    for c, lanes in enumerate(lane_tiles):
        conv_a_lanes(lanes, gate)
        for g, s in hosted[c]:
            in_proj(g, s)
            gate = _zero_after(slot[s][TM - SUBLANES:TM, D - LANES:D])

    conv_a_b, ln_g, ln_b, b_az = row(V_CONV_A_B), row(V_LN_G), row(V_LN_B), bias(C_AZ)
    b_bx, b_cc, b_cx = bias(C_BX), bias(C_CC), bias(C_CX)

    def ln_rows(rows):
        y = _layer_norm(slot[S_CONV_A][rows, :] + conv_a_b, ln_g, ln_b)
        z0_ref[rows, :] = (_silu(y) * _silu(slot[S_AZ][rows, :] + b_az)).astype(bf16)

    def conv_in_rows(rows):
        dst = slice(HIST_S + rows.start, HIST_S + rows.stop)
        bx_ref[dst, :] = slot[S_BX][rows, :] + b_bx
        cc_ref[dst, :] = (slot[S_CC][rows, :] + b_cc) * (slot[S_CX][rows, :] + b_cx)
    _rows(ln_rows, conv_in_rows)

    def small_conv(src_ref, v_row, k_width, bias_row, emit_pair):
        for lanes in lane_tiles:
            w_tiles = [jnp.broadcast_to(vec_ref[v_row + k_width - 1 - j:v_row + k_width - j, lanes],
                                        (SUBLANES, LANES)) for j in range(k_width)]
            bias_tile = None if bias_row is None else vec_ref[bias_row:bias_row + 1, lanes]
            held = []

            def emit(m, out):
                if bias_tile is not None:
                    out = out + bias_tile
                held.append(out)
                if len(held) == 2:
                    emit_pair(slice(SUBLANES * (m - 1), SUBLANES * (m + 1)), lanes,
                              jnp.concatenate(held, axis=0))
                    held.clear()
            _conv_rows_narrow(src_ref, HIST_S, n_blocks, lanes, k_width, w_tiles, emit)

    def emit_xb(rows, lanes, pair):
        slot[S_XB][rows, lanes] = pair
        z1_ref[rows, lanes] = pair.astype(bf16)
    project(z0_ref, w_a_ref, slice(None), slot[S_YA])
    small_conv(bx_ref, V_CONV_B_W, KB, V_CONV_B_B, emit_xb)

    for g in range(N_RG_GROUPS):
        lanes = slice(g * RG_W, (g + 1) * RG_W)
        xg = z1_ref[:, lanes]
        slot[S_R][:, lanes] = jnp.dot(xg, w_rga_ref[g], preferred_element_type=f32)
        slot[S_I][:, lanes] = jnp.dot(xg, w_rgx_ref[g], preferred_element_type=f32)
    b_g0 = bias(C_G0)

    def merge_a_rows(rows):
        yacc_ref[rows, :] = _sigmoid(slot[S_G0][rows, :] + b_g0) * slot[S_YA][rows, :]
    _rows(merge_a_rows)

    in_proj(C_G2, S_G2)
    softplus_neg_lam = _softplus(-row(V_LAMBDA))
    rg_b_a, rg_b_x = row(V_RG_B_A), row(V_RG_B_X)

    def rg_rows(rows):
        a, u = _lru_coeffs(_sigmoid(slot[S_R][rows, :] + rg_b_a), _sigmoid(slot[S_I][rows, :] + rg_b_x),
                           softplus_neg_lam, slot[S_XB][rows, :])
        slot[S_A][rows, :] = a
        slot[S_U][rows, :] = u
    _rows(rg_rows)

    sub = lax.broadcasted_iota(jnp.int32, (SUBLANES, LANES), 0)
    for lanes in lane_tiles:
        h = h_ref[:, lanes]
        for m in range(n_blocks):
            rows = slice(SUBLANES * m, SUBLANES * (m + 1))
            a = slot[S_A][rows, lanes]
            u = slot[S_U][rows, lanes]
            for s in (1, 2, 4):
                keep = sub >= s
                a_sh = pltpu.roll(a, s, 0)
                u_sh = pltpu.roll(u, s, 0)
                u = jnp.where(keep, a * u_sh + u, u)
                a = jnp.where(keep, a * a_sh, a)
            hs = a * h + u
            slot[S_H][rows, lanes] = hs
            h = jnp.broadcast_to(hs[SUBLANES - 1:SUBLANES, :], (SUBLANES, LANES))
        h_ref[:, lanes] = h

    b_bz = bias(C_BZ)

    def bz_rows(rows):
        z0_ref[rows, :] = (slot[S_H][rows, :] * _silu(slot[S_BZ][rows, :] + b_bz)).astype(bf16)
    _rows(bz_rows)

    def emit_conv_c(rows, lanes, pair):
        slot[S_CONV_C][rows, lanes] = pair
    project(z0_ref, w_b_ref, slice(None), slot[S_YB])
    small_conv(cc_ref, V_CONV_C_W, KC, None, emit_conv_c)

    b_cb, b_cz = bias(C_CB), bias(C_CZ)

    def cz_rows(rows):
        zc = ((slot[S_CB][rows, :] + b_cb) * slot[S_CONV_C][rows, :]) * _silu(slot[S_CZ][rows, :] + b_cz)
        z1_ref[rows, :] = zc.astype(bf16)
    _rows(cz_rows)

    b_g1, b_g2 = bias(C_G1), bias(C_G2)
    project(z1_ref, w_c_ref, slice(None), slot[S_YC])

    def merge_b_rows(rows):
        slot[S_MERGED_AB][rows, :] = yacc_ref[rows, :] + _sigmoid(slot[S_G1][rows, :] + b_g1) * slot[S_YB][rows, :]
    _rows(merge_b_rows)

    def merge_c_rows(rows):
        merged = slot[S_MERGED_AB][rows, :] + _sigmoid(slot[S_G2][rows, :] + b_g2) * slot[S_YC][rows, :]
        z0_ref[rows, :] = merged.astype(bf16)
    _rows(merge_c_rows)

    project(z0_ref, w_o_ref, slice(None), slot[S_OUT])
    g_post = row(V_NORM_POST)

    def out_rows(rows):
        y_ref[rows, :] = x_ref[rows, :] + _rms_scale(slot[S_OUT][rows, :]) * g_post
    _rows(out_rows)

    @pl.when(t == pl.num_programs(1) - 1)
    def _():
        na_ref[...] = ua_ref[HIST_A + TM - (KA - 1):HIST_A + TM, :]
        nb_ref[...] = bx_ref[HIST_S + TM - (KB - 1):HIST_S + TM, :]
        nc_ref[...] = cc_ref[HIST_S + TM - (KC - 1):HIST_S + TM, :]
        nh_ref[...] = h_ref[0:1, :]

    ua_ref[0:HIST_A, :] = ua_ref[TM:TM + HIST_A, :]
    bx_ref[0:HIST_S, :] = bx_ref[TM:TM + HIST_S, :]
    cc_ref[0:HIST_S, :] = cc_ref[TM:TM + HIST_S, :]


def _resident(shape, layer):
    nd = len(shape)
    return pl.BlockSpec((None,) + tuple(shape[1:]), lambda *_: (layer,) + (0,) * (nd - 1),
                        pipeline_mode=pl.Buffered(1))


def _weight_specs(weights, layer):
    return [_resident(w.shape, layer) for w in weights]


def _prompt_layer(x, weights, layer):
    bsz, seq, _ = x.shape
    assert seq % TM == 0 and TM % RC == 0 and TM % CONV_A_ROWS == 0
    n_tiles = seq // TM
    f32 = jnp.float32
    out_shape = (
        jax.ShapeDtypeStruct((bsz, seq, D), f32),
        jax.ShapeDtypeStruct((bsz, KA - 1, D), f32),
        jax.ShapeDtypeStruct((bsz, KB - 1, D), f32),
        jax.ShapeDtypeStruct((bsz, 1, D), f32),
        jax.ShapeDtypeStruct((bsz, KC - 1, D), f32),
    )

    per_seq = lambda rows: pl.BlockSpec((None, rows, D), lambda b, t: (b, 0, 0))
    return pl.pallas_call(
        _prompt_kernel,
        grid=(bsz, n_tiles),
        in_specs=[pl.BlockSpec((None, TM, D), lambda b, t: (b, t, 0))] + _weight_specs(weights, layer),
        out_specs=(pl.BlockSpec((None, TM, D), lambda b, t: (b, t, 0)),
                   per_seq(KA - 1), per_seq(KB - 1), per_seq(1), per_seq(KC - 1)),
        out_shape=out_shape,
        scratch_shapes=[
            pltpu.VMEM((TM, D), jnp.bfloat16),
            pltpu.VMEM((TM, D), jnp.bfloat16),
            pltpu.VMEM((TM, D), jnp.bfloat16),
            pltpu.VMEM((TM, D), f32),
            pltpu.VMEM((KA * SUBLANES, D), f32),
            pltpu.VMEM((HIST_A + TM, D), f32),
            pltpu.VMEM((HIST_S + TM, D), f32),
            pltpu.VMEM((HIST_S + TM, D), f32),
            pltpu.VMEM((SUBLANES, D), f32),
        ] + [pltpu.VMEM((TM, D), f32) for _ in range(N_SLOTS)],
        compiler_params=pltpu.CompilerParams(
            dimension_semantics=("arbitrary", "arbitrary"), vmem_limit_bytes=VMEM_LIMIT_BYTES),
        name=f"prompt_layer{layer}",
    )(x, *weights)


class _Refs:
    def __init__(self, vec, w_in, w_a, w_b, w_c, w_o, w_rga, w_rgx, hb, p, q, z, yacc, wexp):
        self.vec, self.w_in, self.w_a, self.w_b, self.w_c, self.w_o = vec, w_in, w_a, w_b, w_c, w_o
        self.w_rga, self.w_rgx = w_rga, w_rgx
        self.hb, self.p, self.q, self.z, self.yacc, self.wexp = hb, p, q, z, yacc, wexp

    def row(self, i):
        return self.vec[i:i + 1, :]

    def bias(self, g):
        return self.vec[V_B_IN + g:V_B_IN + g + 1, :]


def _in_proj(r, n_rows, g, dst_g):
    r.p[0:n_rows, _cols(dst_g)] = jnp.dot(
        r.hb[0:n_rows, :], r.w_in[:, _cols(g)], preferred_element_type=jnp.float32)


def _layer_body(r, n_rows, x_load, y_store, glu_store, conv_a, bx_store, conv_b, scan, cc_store, conv_c):
    f32, bf16 = jnp.float32, jnp.bfloat16

    g_pre = r.row(V_NORM_PRE)

    def norm_body(r0):
        r.hb[pl.ds(r0, RC), :] = (_rms_scale(x_load(r0)) * g_pre).astype(bf16)
    _for_chunks(n_rows, RC, norm_body)

    def merge(first, last):
        def body(r0, bias):
            rows = pl.ds(r0, RC)
            contrib = _sigmoid(r.p[rows, _cols(1)] + bias) * r.p[rows, _cols(0)]
            if not first:
                contrib = r.yacc[rows, :] + contrib
            if last:
                r.z[rows, :] = contrib.astype(bf16)
            else:
                r.yacc[rows, :] = contrib
        return body

    def branch_out(w_ref, gate_g, first=False, last=False):
        r.p[0:n_rows, _cols(0)] = jnp.dot(r.z[0:n_rows, :], w_ref[...], preferred_element_type=f32)
        _in_proj(r, n_rows, gate_g, 1)
        bias = r.bias(gate_g)
        body = merge(first, last)
        _for_chunks(n_rows, RC, lambda r0: body(r0, bias))

    for j, g in enumerate((C_AV, C_AG, C_AZ)):
        _in_proj(r, n_rows, g, j)
    b_av, b_ag, b_az = r.bias(C_AV), r.bias(C_AG), r.bias(C_AZ)

    def glu_body(r0):
        rows = pl.ds(r0, RC)
        glu_store(r0, (r.p[rows, _cols(0)] + b_av) * _sigmoid(r.p[rows, _cols(1)] + b_ag))
    _for_chunks(n_rows, RC, glu_body)

    conv_a()
    conv_a_b, ln_g, ln_b = r.row(V_CONV_A_B), r.row(V_LN_G), r.row(V_LN_B)

    def ln_body(r0):
        rows = pl.ds(r0, RC)
        y = _layer_norm(r.p[rows, _cols(0)] + conv_a_b, ln_g, ln_b)
        r.z[rows, :] = (_silu(y) * _silu(r.p[rows, _cols(2)] + b_az)).astype(bf16)
    _for_chunks(n_rows, RC, ln_body)
    branch_out(r.w_a, C_G0, first=True)

    for j, g in enumerate((C_BX, C_BZ)):
        _in_proj(r, n_rows, g, j)
    b_bx, b_bz = r.bias(C_BX), r.bias(C_BZ)

    def bx_body(r0):
        bx_store(r0, r.p[pl.ds(r0, RC), _cols(0)] + b_bx)
    _for_chunks(n_rows, RC, bx_body)

    conv_b()

    def xb_cast_body(r0):
        rows = pl.ds(r0, RC)
        r.z[rows, :] = r.p[rows, _cols(2)].astype(bf16)
    _for_chunks(n_rows, RC, xb_cast_body)

    for g in range(N_RG_GROUPS):
        lanes = slice(g * RG_W, (g + 1) * RG_W)
        xg = r.z[0:n_rows, lanes]
        r.q[0:n_rows, lanes] = jnp.dot(xg, r.w_rga[g], preferred_element_type=f32)
        r.q[0:n_rows, slice(D + g * RG_W, D + (g + 1) * RG_W)] = jnp.dot(
            xg, r.w_rgx[g], preferred_element_type=f32)

    softplus_neg_lam = _softplus(-r.row(V_LAMBDA))
    rg_b_a, rg_b_x = r.row(V_RG_B_A), r.row(V_RG_B_X)

    def rg_body(r0):
        rows = pl.ds(r0, RC)
        a, u = _lru_coeffs(_sigmoid(r.q[rows, _cols(0)] + rg_b_a), _sigmoid(r.q[rows, _cols(1)] + rg_b_x),
                           softplus_neg_lam, r.p[rows, _cols(2)])
        r.q[rows, _cols(0)] = a
        r.q[rows, _cols(1)] = u
    _for_chunks(n_rows, RC, rg_body)

    scan()

    def bz_body(r0):
        rows = pl.ds(r0, RC)
        r.z[rows, :] = (r.q[rows, _cols(0)] * _silu(r.p[rows, _cols(1)] + b_bz)).astype(bf16)
    _for_chunks(n_rows, RC, bz_body)
    branch_out(r.w_b, C_G1)

    for j, g in enumerate((C_CB, C_CC, C_CX, C_CZ)):
        _in_proj(r, n_rows, g, j)
    b_cb, b_cc, b_cx, b_cz = r.bias(C_CB), r.bias(C_CC), r.bias(C_CX), r.bias(C_CZ)

    def cc_body(r0):
        rows = pl.ds(r0, RC)
        cc_store(r0, (r.p[rows, _cols(1)] + b_cc) * (r.p[rows, _cols(2)] + b_cx))
    _for_chunks(n_rows, RC, cc_body)

    conv_c()

    def cz_body(r0):
        rows = pl.ds(r0, RC)
        zc = ((r.p[rows, _cols(0)] + b_cb) * r.p[rows, _cols(1)]) * _silu(r.p[rows, _cols(3)] + b_cz)
        r.z[rows, :] = zc.astype(bf16)
    _for_chunks(n_rows, RC, cz_body)
    branch_out(r.w_c, C_G2, last=True)

    r.p[0:n_rows, _cols(0)] = jnp.dot(r.z[0:n_rows, :], r.w_o[...], preferred_element_type=f32)
    g_post = r.row(V_NORM_POST)

    def out_body(r0):
        y_store(r0, x_load(r0) + _rms_scale(r.p[pl.ds(r0, RC), _cols(0)]) * g_post)
    _for_chunks(n_rows, RC, out_body)


def _sample_kernel(n_t, x_ref, sa_ref, sb_ref, sh_ref, sc_ref,
                   vec_ref, w_in_ref, w_a_ref, w_b_ref, w_c_ref, w_o_ref, w_rga_ref, w_rgx_ref,
                   y_ref, ua_out_ref, nb_ref, nh_ref, nc_ref,
                   hb_ref, p_ref, q_ref, z_ref, yacc_ref, wexp_ref, ua_ref, bx_ref, cc_ref):
    bs = SAMPLE_BLOCK
    n_rows = n_t * bs
    r = _Refs(vec_ref, w_in_ref, w_a_ref, w_b_ref, w_c_ref, w_o_ref, w_rga_ref, w_rgx_ref,
              hb_ref, p_ref, q_ref, z_ref, yacc_ref, wexp_ref)
    _expand_conv_a_weights(vec_ref, wexp_ref)

    def flat_store(ref):
        def store(r0, val):
            ref[pl.ds(r0, RC), :] = val
        return store

    def slab_conv(state_ref, new_ref, k_width, tap_weight, dst_g, bias_row):
        n_state = k_width - 1

        def run():
            def body(r0):
                rows = pl.ds(r0, SUBLANES)
                for t in range(n_t):
                    acc = None
                    for k in range(k_width):
                        i = t + k
                        if i < n_state:
                            full = state_ref[i, rows, :]
                        else:
                            full = new_ref[pl.ds((i - n_state) * bs + r0, SUBLANES), :]
                        term = tap_weight(k) * full
                        acc = term if acc is None else acc + term
                    if bias_row is not None:
                        acc = acc + r.row(bias_row)
                    p_ref[pl.ds(t * bs + r0, SUBLANES), _cols(dst_g)] = acc
            _for_chunks(bs, SUBLANES, body)
        return run

    def vec_tap(v_row):
        return lambda k: r.row(v_row + k)

    def scan():
        def body(r0):
            rows = pl.ds(r0, SUBLANES)
            h = sh_ref[rows, :]
            for t in range(n_t):
                trows = pl.ds(t * bs + r0, SUBLANES)
                h = q_ref[trows, _cols(0)] * h + q_ref[trows, _cols(1)]
                q_ref[trows, _cols(0)] = h
            nh_ref[rows, :] = h
        _for_chunks(bs, SUBLANES, body)

    def y_store(r0, val):
        y_ref[pl.ds(r0, RC), :] = val

    _layer_body(
        r, n_rows,
        x_load=lambda r0: x_ref[pl.ds(r0, RC), :],
        y_store=y_store,
        glu_store=flat_store(ua_ref),
        conv_a=slab_conv(sa_ref, ua_ref, KA,
                         lambda k: wexp_ref[k * SUBLANES:(k + 1) * SUBLANES, :], 0, None),
        bx_store=flat_store(bx_ref),
        conv_b=slab_conv(sb_ref, bx_ref, KB, vec_tap(V_CONV_B_W), 2, V_CONV_B_B),
        scan=scan,
        cc_store=flat_store(cc_ref),
        conv_c=slab_conv(sc_ref, cc_ref, KC, vec_tap(V_CONV_C_W), 1, None))

    ua_out_ref[...] = ua_ref[...]
    nb_ref[...] = bx_ref[(n_t - (KB - 1)) * bs:n_t * bs, :]
    nc_ref[...] = cc_ref[(n_t - (KC - 1)) * bs:n_t * bs, :]


def _sample_layer(x_tm, sa_tm, sb_tm, sh, sc_tm, weights, layer):
    n_blk, n_rows, _ = x_tm.shape
    bs = SAMPLE_BLOCK
    n_t = n_rows // bs
    bsz = n_blk * bs
    assert n_t >= KB - 1 and n_t >= KC - 1 and bs % RC == 0
    f32 = jnp.float32
    state = lambda k: pl.BlockSpec((None, k, bs, D), lambda i: (layer, 0, i, 0))
    rows_blk = lambda n: pl.BlockSpec((None, n, D), lambda i: (i, 0, 0))
    out_shape = (
        jax.ShapeDtypeStruct((n_blk, n_rows, D), f32),
        jax.ShapeDtypeStruct((n_blk, n_rows, D), f32),
        jax.ShapeDtypeStruct((n_blk, (KB - 1) * bs, D), f32),
        jax.ShapeDtypeStruct((bsz, D), f32),
        jax.ShapeDtypeStruct((n_blk, (KC - 1) * bs, D), f32),
    )
    return pl.pallas_call(
        functools.partial(_sample_kernel, n_t),
        grid=(n_blk,),
        in_specs=[rows_blk(n_rows), state(KA - 1), state(KB - 1),
                  pl.BlockSpec((None, bs, D), lambda i: (layer, i, 0)), state(KC - 1)]
        + _weight_specs(weights, layer),
        out_specs=(rows_blk(n_rows), rows_blk(n_rows), rows_blk((KB - 1) * bs),
                   pl.BlockSpec((bs, D), lambda i: (i, 0)), rows_blk((KC - 1) * bs)),
        out_shape=out_shape,
        scratch_shapes=[
            pltpu.VMEM((n_rows, D), jnp.bfloat16),
            pltpu.VMEM((n_rows, 4 * D), f32),
            pltpu.VMEM((n_rows, 2 * D), f32),
            pltpu.VMEM((n_rows, D), jnp.bfloat16),
            pltpu.VMEM((n_rows, D), f32),
            pltpu.VMEM((KA * SUBLANES, D), f32),
            pltpu.VMEM((n_rows, D), f32),
            pltpu.VMEM((n_rows, D), f32),
            pltpu.VMEM((n_rows, D), f32),
        ],
        compiler_params=pltpu.CompilerParams(
            dimension_semantics=("arbitrary",), vmem_limit_bytes=VMEM_LIMIT_BYTES),
        name=f"sample_layer{layer}",
    )(x_tm, sa_tm, sb_tm, sh, sc_tm, *weights)


def _block_diag_groups(w):
    depth, n_blocks, blk, _ = w.shape
    per = n_blocks // N_RG_GROUPS
    w = w.reshape(depth, N_RG_GROUPS, per, blk, blk)
    eye = jnp.eye(per, dtype=w.dtype)
    dense = jnp.einsum('lgaij,ab->lgaibj', w, eye)
    return dense.reshape(depth, N_RG_GROUPS, per * blk, per * blk)


def _to_time_major_blocks(x):
    bsz, n_t, _ = x.shape
    bs = SAMPLE_BLOCK
    return x.reshape(bsz // bs, bs, n_t, D).transpose(0, 2, 1, 3).reshape(bsz // bs, n_t * bs, D)


def _from_time_major_blocks(x, n_t):
    n_blk = x.shape[0]
    bs = SAMPLE_BLOCK
    return x.reshape(n_blk, n_t, bs, D).transpose(0, 2, 1, 3).reshape(n_blk * bs, n_t, D)


def kernel(x_prompt, x_sample, state_conv_a, state_conv_b, state_lru, state_conv_c, norm_pre, norm_post, w_in, b_in, conv_a_w, conv_a_b, ln_a_g, ln_a_b, w_a_out, conv_b_w, conv_b_b, rg_w_a, rg_b_a, rg_w_x, rg_b_x, rg_lambda, w_b_out, conv_c_w, w_c_out, w_o):
    bf16 = jnp.bfloat16
    depth = w_in.shape[0]
    n_t = x_sample.shape[1]

    rows = [norm_pre, norm_post, conv_a_b, ln_a_g, ln_a_b, conv_b_b, rg_b_a, rg_b_x, rg_lambda]
    vec = jnp.concatenate(
        [v[:, None, :] for v in rows]
        + [b_in.reshape(depth, N_COL_GROUPS, D), conv_a_w, conv_b_w, conv_c_w], axis=1)
    vec = jnp.pad(vec, ((0, 0), (0, NV - vec.shape[1]), (0, 0)))
    weights = (vec, w_in.astype(bf16), w_a_out.astype(bf16), w_b_out.astype(bf16),
               w_c_out.astype(bf16), w_o.astype(bf16),
               _block_diag_groups(rg_w_a).astype(bf16), _block_diag_groups(rg_w_x).astype(bf16))

    xs = _to_time_major_blocks(x_sample)
    sa_tm = state_conv_a.transpose(0, 2, 1, 3)
    sb_tm = state_conv_b.transpose(0, 2, 1, 3)
    sc_tm = state_conv_c.transpose(0, 2, 1, 3)

    xp = x_prompt
    pa, pb, ph, pc = [], [], [], []
    sa, sb, sh, sc = [], [], [], []
    for l in range(depth):
        xp, na, nb, nh, nc = _prompt_layer(xp, weights, l)
        pa.append(na); pb.append(nb); ph.append(nh[:, 0, :]); pc.append(nc)

        xs, ua_new, nb, nh, nc = _sample_layer(xs, sa_tm, sb_tm, state_lru, sc_tm, weights, l)
        ua_new = _from_time_major_blocks(ua_new, n_t)
        sa.append(jnp.concatenate([state_conv_a[l], ua_new], axis=1)[:, -(KA - 1):])
        sb.append(_from_time_major_blocks(nb, KB - 1))
        sh.append(nh)
        sc.append(_from_time_major_blocks(nc, KC - 1))

    return (xp, _from_time_major_blocks(xs, n_t),
            jnp.stack(pa), jnp.stack(pb), jnp.stack(ph), jnp.stack(pc),
            jnp.stack(sa), jnp.stack(sb), jnp.stack(sh), jnp.stack(sc))
```

```python
import functools

import jax
import jax.numpy as jnp
from jax import lax
from jax.experimental import pallas as pl
from jax.experimental.pallas import tpu as pltpu

D = 1024
DEPTH = 4
EPS = 1e-6
LRU_C = 8.0
KA, KB, KC = 31, 4, 3
N_RG_GROUPS = 4
RG_W = D // N_RG_GROUPS

SUBLANES = 8
LANES = 128
N_LANE_TILES = D // LANES

C_AV, C_AG, C_AZ, C_BX, C_BZ, C_CB, C_CC, C_CX, C_CZ, C_G0, C_G1, C_G2 = range(12)
N_COL_GROUPS = 12

V_NORM_PRE, V_NORM_POST, V_CONV_A_B, V_LN_G, V_LN_B, V_CONV_B_B, V_RG_B_A, V_RG_B_X, V_LAMBDA = range(9)
V_B_IN = 9
V_CONV_A_W = V_B_IN + N_COL_GROUPS
V_CONV_B_W = V_CONV_A_W + KA
V_CONV_C_W = V_CONV_B_W + KB
NV = 64
assert V_CONV_C_W + KC <= NV

TM = 256
HIST_A = 32
HIST_S = 8
RC = 16
CONV_A_ROWS = 128
N_SLOTS = 10
SAMPLE_BLOCK = 32
VMEM_LIMIT_BYTES = 60 * 1024 * 1024


def _sigmoid(x):
    return jax.nn.sigmoid(x)


def _silu(x):
    return x * jax.nn.sigmoid(x)


def _softplus(x):
    return jnp.maximum(x, 0.0) + jnp.log1p(jnp.exp(-jnp.abs(x)))


def _lru_coeffs(rgate, igate, softplus_neg_lam, xb):
    log_a = (-LRU_C * rgate) * softplus_neg_lam
    a = jnp.exp(log_a)
    mult = jnp.sqrt(-jnp.tanh(log_a) * (1.0 + a * a))
    return a, mult * (igate * xb)


def _layer_norm(v, gain, bias):
    mu = jnp.mean(v, axis=-1, keepdims=True)
    d = v - mu
    var = jnp.mean(d * d, axis=-1, keepdims=True)
    return (d * lax.rsqrt(var + EPS)) * gain + bias


def _rms_scale(v):
    return v * lax.rsqrt(jnp.mean(v * v, axis=-1, keepdims=True) + EPS)


def _for_chunks(n_rows, rc, body):
    for r0 in range(0, n_rows, rc):
        body(r0)


def _cols(g, n=1):
    return slice(g * D, (g + n) * D)


def _expand_conv_a_weights(vec_ref, wexp_ref):
    for k in range(KA):
        wexp_ref[k * SUBLANES:(k + 1) * SUBLANES, :] = jnp.broadcast_to(
            vec_ref[V_CONV_A_W + k:V_CONV_A_W + k + 1, :], (SUBLANES, D))


S_AV, S_AG, S_AZ, S_BX, S_BZ, S_CB, S_CC, S_CX, S_CZ, S_G0 = range(N_SLOTS)
S_CONV_A = S_R = S_H = S_OUT = S_AV
S_G1 = S_AG
S_I = S_MERGED_AB = S_AZ
S_XB = S_YB = S_BX
S_YA = S_A = S_CONV_C = S_CC
S_U = S_YC = S_CX
S_G2 = S_G0


def _rows(*row_bodies):
    for r0 in range(0, TM, RC):
        for body in row_bodies:
            body(slice(r0, r0 + RC))


def _zero_after(tile):
    bits = pltpu.bitcast(tile, jnp.uint32)
    bits = lax.shift_right_logical(lax.shift_right_logical(bits, jnp.uint32(16)), jnp.uint32(16))
    return pltpu.bitcast(bits, jnp.float32)


def _shift_in(sub, cur, prev, s):
    return pltpu.roll(jnp.where(sub < SUBLANES - s, cur, prev), s, 0)


def _conv_rows_wide(src_ref, row0, n_blocks, lanes, k_width, w_tile, emit, gate=None):
    nq = -(-k_width // SUBLANES)
    sub = lax.broadcasted_iota(jnp.int32, (SUBLANES, LANES), 0)
    blocks = {b: src_ref[row0 + SUBLANES * b:row0 + SUBLANES * (b + 1), lanes]
              for b in range(-nq, n_blocks)}
    if gate is not None:
        blocks = {b: v + gate for b, v in blocks.items()}
    acc = [None] * n_blocks
    for s in range(SUBLANES):
        taps = [(q, w_tile(SUBLANES * q + s)) for q in range(nq) if SUBLANES * q + s < k_width]

        def qsum(m):
            total = None
            for q, w in taps:
                term = w * blocks[m - q]
                total = term if total is None else total + term
            return total

        if s == 0:
            for m in range(n_blocks):
                acc[m] = qsum(m)
        else:
            prev = qsum(-1)
            for m in range(n_blocks):
                cur = qsum(m)
                acc[m] = acc[m] + _shift_in(sub, cur, prev, s)
                prev = cur
    for m in range(n_blocks):
        emit(m, acc[m])


def _conv_rows_narrow(src_ref, row0, n_blocks, lanes, k_width, w_tiles, emit):
    sub = lax.broadcasted_iota(jnp.int32, (SUBLANES, LANES), 0)
    before = src_ref[row0 - SUBLANES:row0, lanes]
    prev = [w_tiles[s] * before for s in range(1, k_width)]
    for m in range(n_blocks):
        um = src_ref[row0 + SUBLANES * m:row0 + SUBLANES * (m + 1), lanes]
        out = w_tiles[0] * um
        cur = []
        for s in range(1, k_width):
            term = w_tiles[s] * um
            out = out + _shift_in(sub, term, prev[s - 1], s)
            cur.append(term)
        prev = cur
        emit(m, out)


def _prompt_kernel(x_ref, vec_ref, w_in_ref, w_a_ref, w_b_ref, w_c_ref, w_o_ref, w_rga_ref, w_rgx_ref,
                   y_ref, na_ref, nb_ref, nh_ref, nc_ref,
                   hb_ref, z0_ref, z1_ref, yacc_ref, wexp_ref, ua_ref, bx_ref, cc_ref, h_ref, *slot):
    f32, bf16 = jnp.float32, jnp.bfloat16
    t = pl.program_id(1)
    n_blocks = TM // SUBLANES
    assert len(slot) == N_SLOTS
    lane_tiles = [slice(c * LANES, (c + 1) * LANES) for c in range(N_LANE_TILES)]

    av_ref, ag_ref = slot[S_AV], slot[S_AG]

    def row(i):
        return vec_ref[i:i + 1, :]

    def bias(g):
        return row(V_B_IN + g)

    def project(lhs_ref, w_ref, w_cols, dst_ref):
        dst_ref[...] = jnp.dot(lhs_ref[...], w_ref[:, w_cols], preferred_element_type=f32)

    def in_proj(group, dst_slot):
        project(hb_ref, w_in_ref, _cols(group), slot[dst_slot])

    @pl.when(t == 0)
    def _():
        ua_ref[0:HIST_A, :] = jnp.zeros((HIST_A, D), f32)
        bx_ref[0:HIST_S, :] = jnp.zeros((HIST_S, D), f32)
        cc_ref[0:HIST_S, :] = jnp.zeros((HIST_S, D), f32)
        h_ref[...] = jnp.zeros((SUBLANES, D), f32)
        _expand_conv_a_weights(vec_ref, wexp_ref)

    g_pre = row(V_NORM_PRE)

    def norm_rows(rows):
        hb_ref[rows, :] = (_rms_scale(x_ref[rows, :]) * g_pre).astype(bf16)
    _rows(norm_rows)

    in_proj(C_AV, S_AV)
    in_proj(C_AG, S_AG)
    in_proj(C_AZ, S_AZ)
    in_proj(C_BX, S_BX)
    b_av, b_ag = bias(C_AV), bias(C_AG)

    def glu_rows(rows):
        ua_ref[HIST_A + rows.start:HIST_A + rows.stop, :] = (
            (av_ref[rows, :] + b_av) * _sigmoid(ag_ref[rows, :] + b_ag))
    _rows(glu_rows)

    def conv_a_lanes(lanes, gate):
        def w_tile(j):
            k = KA - 1 - j
            return wexp_ref[k * SUBLANES:(k + 1) * SUBLANES, lanes]

        for row0 in range(0, TM, CONV_A_ROWS):
            def emit(m, out):
                slot[S_CONV_A][row0 + SUBLANES * m:row0 + SUBLANES * (m + 1), lanes] = out
            _conv_rows_wide(ua_ref, HIST_A + row0, CONV_A_ROWS // SUBLANES, lanes, KA, w_tile, emit, gate)

    hosted = ((C_BZ, S_BZ), (C_CB, S_CB), (C_CC, S_CC), (C_CX, S_CX), (C_CZ, S_CZ), (C_G0, S_G0), (C_G1, S_G1))
    gate = None
    for c, lanes in enumerate(lane_tiles):
        conv_a_lanes(lanes, gate)
        if c < len(hosted):
            g, s = hosted[c]
            in_proj(g, s)
            gate = _zero_after(slot[s][TM - SUBLANES:TM, D - LANES:D])

    conv_a_b, ln_g, ln_b, b_az = row(V_CONV_A_B), row(V_LN_G), row(V_LN_B), bias(C_AZ)
    b_bx, b_cc, b_cx = bias(C_BX), bias(C_CC), bias(C_CX)

    def ln_rows(rows):
        y = _layer_norm(slot[S_CONV_A][rows, :] + conv_a_b, ln_g, ln_b)
        z0_ref[rows, :] = (_silu(y) * _silu(slot[S_AZ][rows, :] + b_az)).astype(bf16)

    def conv_in_rows(rows):
        dst = slice(HIST_S + rows.start, HIST_S + rows.stop)
        bx_ref[dst, :] = slot[S_BX][rows, :] + b_bx
        cc_ref[dst, :] = (slot[S_CC][rows, :] + b_cc) * (slot[S_CX][rows, :] + b_cx)
    _rows(ln_rows, conv_in_rows)

    def small_conv(src_ref, v_row, k_width, bias_row, emit_pair):
        for lanes in lane_tiles:
            w_tiles = [jnp.broadcast_to(vec_ref[v_row + k_width - 1 - j:v_row + k_width - j, lanes],
                                        (SUBLANES, LANES)) for j in range(k_width)]
            bias_tile = None if bias_row is None else vec_ref[bias_row:bias_row + 1, lanes]
            held = []

            def emit(m, out):
                if bias_tile is not None:
                    out = out + bias_tile
                held.append(out)
                if len(held) == 2:
                    emit_pair(slice(SUBLANES * (m - 1), SUBLANES * (m + 1)), lanes,
                              jnp.concatenate(held, axis=0))
                    held.clear()
            _conv_rows_narrow(src_ref, HIST_S, n_blocks, lanes, k_width, w_tiles, emit)

    def emit_xb(rows, lanes, pair):
        slot[S_XB][rows, lanes] = pair
        z1_ref[rows, lanes] = pair.astype(bf16)
    project(z0_ref, w_a_ref, slice(None), slot[S_YA])
    small_conv(bx_ref, V_CONV_B_W, KB, V_CONV_B_B, emit_xb)

    for g in range(N_RG_GROUPS):
        lanes = slice(g * RG_W, (g + 1) * RG_W)
        xg = z1_ref[:, lanes]
        slot[S_R][:, lanes] = jnp.dot(xg, w_rga_ref[g], preferred_element_type=f32)
        slot[S_I][:, lanes] = jnp.dot(xg, w_rgx_ref[g], preferred_element_type=f32)
    b_g0 = bias(C_G0)

    def merge_a_rows(rows):
        yacc_ref[rows, :] = _sigmoid(slot[S_G0][rows, :] + b_g0) * slot[S_YA][rows, :]
    _rows(merge_a_rows)

    in_proj(C_G2, S_G2)
    softplus_neg_lam = _softplus(-row(V_LAMBDA))
    rg_b_a, rg_b_x = row(V_RG_B_A), row(V_RG_B_X)

    def rg_rows(rows):
        a, u = _lru_coeffs(_sigmoid(slot[S_R][rows, :] + rg_b_a), _sigmoid(slot[S_I][rows, :] + rg_b_x),
                           softplus_neg_lam, slot[S_XB][rows, :])
        slot[S_A][rows, :] = a
        slot[S_U][rows, :] = u
    _rows(rg_rows)

    sub = lax.broadcasted_iota(jnp.int32, (SUBLANES, LANES), 0)
    for lanes in lane_tiles:
        h = h_ref[:, lanes]
        for m in range(n_blocks):
            rows = slice(SUBLANES * m, SUBLANES * (m + 1))
            a = slot[S_A][rows, lanes]
            u = slot[S_U][rows, lanes]
            for s in (1, 2, 4):
                keep = sub >= s
                a_sh = pltpu.roll(a, s, 0)
                u_sh = pltpu.roll(u, s, 0)
                u = jnp.where(keep, a * u_sh + u, u)
                a = jnp.where(keep, a * a_sh, a)
            hs = a * h + u
            slot[S_H][rows, lanes] = hs
            h = jnp.broadcast_to(hs[SUBLANES - 1:SUBLANES, :], (SUBLANES, LANES))
        h_ref[:, lanes] = h

    b_bz = bias(C_BZ)

    def bz_rows(rows):
        z0_ref[rows, :] = (slot[S_H][rows, :] * _silu(slot[S_BZ][rows, :] + b_bz)).astype(bf16)
    _rows(bz_rows)

    def emit_conv_c(rows, lanes, pair):
        slot[S_CONV_C][rows, lanes] = pair
    project(z0_ref, w_b_ref, slice(None), slot[S_YB])
    small_conv(cc_ref, V_CONV_C_W, KC, None, emit_conv_c)

    b_cb, b_cz = bias(C_CB), bias(C_CZ)

    def cz_rows(rows):
        zc = ((slot[S_CB][rows, :] + b_cb) * slot[S_CONV_C][rows, :]) * _silu(slot[S_CZ][rows, :] + b_cz)
        z1_ref[rows, :] = zc.astype(bf16)
    _rows(cz_rows)

    b_g1, b_g2 = bias(C_G1), bias(C_G2)
    project(z1_ref, w_c_ref, slice(None), slot[S_YC])

    def merge_b_rows(rows):
        slot[S_MERGED_AB][rows, :] = yacc_ref[rows, :] + _sigmoid(slot[S_G1][rows, :] + b_g1) * slot[S_YB][rows, :]
    _rows(merge_b_rows)

    def merge_c_rows(rows):
        merged = slot[S_MERGED_AB][rows, :] + _sigmoid(slot[S_G2][rows, :] + b_g2) * slot[S_YC][rows, :]
        z0_ref[rows, :] = merged.astype(bf16)
    _rows(merge_c_rows)

    project(z0_ref, w_o_ref, slice(None), slot[S_OUT])
    g_post = row(V_NORM_POST)

    def out_rows(rows):
        y_ref[rows, :] = x_ref[rows, :] + _rms_scale(slot[S_OUT][rows, :]) * g_post
    _rows(out_rows)

    @pl.when(t == pl.num_programs(1) - 1)
    def _():
        na_ref[...] = ua_ref[HIST_A + TM - (KA - 1):HIST_A + TM, :]
        nb_ref[...] = bx_ref[HIST_S + TM - (KB - 1):HIST_S + TM, :]
        nc_ref[...] = cc_ref[HIST_S + TM - (KC - 1):HIST_S + TM, :]
        nh_ref[...] = h_ref[0:1, :]

    ua_ref[0:HIST_A, :] = ua_ref[TM:TM + HIST_A, :]
    bx_ref[0:HIST_S, :] = bx_ref[TM:TM + HIST_S, :]
    cc_ref[0:HIST_S, :] = cc_ref[TM:TM + HIST_S, :]


def _resident(shape, layer):
    nd = len(shape)
    return pl.BlockSpec((None,) + tuple(shape[1:]), lambda *_: (layer,) + (0,) * (nd - 1),
                        pipeline_mode=pl.Buffered(1))


def _weight_specs(weights, layer):
    return [_resident(w.shape, layer) for w in weights]


def _prompt_layer(x, weights, layer):
    bsz, seq, _ = x.shape
    assert seq % TM == 0 and TM % RC == 0 and TM % CONV_A_ROWS == 0
    n_tiles = seq // TM
    f32 = jnp.float32
    out_shape = (
        jax.ShapeDtypeStruct((bsz, seq, D), f32),
        jax.ShapeDtypeStruct((bsz, KA - 1, D), f32),
        jax.ShapeDtypeStruct((bsz, KB - 1, D), f32),
        jax.ShapeDtypeStruct((bsz, 1, D), f32),
        jax.ShapeDtypeStruct((bsz, KC - 1, D), f32),
    )

    per_seq = lambda rows: pl.BlockSpec((None, rows, D), lambda b, t: (b, 0, 0))
    return pl.pallas_call(
        _prompt_kernel,
        grid=(bsz, n_tiles),
        in_specs=[pl.BlockSpec((None, TM, D), lambda b, t: (b, t, 0))] + _weight_specs(weights, layer),
        out_specs=(pl.BlockSpec((None, TM, D), lambda b, t: (b, t, 0)),
                   per_seq(KA - 1), per_seq(KB - 1), per_seq(1), per_seq(KC - 1)),
        out_shape=out_shape,
        scratch_shapes=[
            pltpu.VMEM((TM, D), jnp.bfloat16),
            pltpu.VMEM((TM, D), jnp.bfloat16),
            pltpu.VMEM((TM, D), jnp.bfloat16),
            pltpu.VMEM((TM, D), f32),
            pltpu.VMEM((KA * SUBLANES, D), f32),
            pltpu.VMEM((HIST_A + TM, D), f32),
            pltpu.VMEM((HIST_S + TM, D), f32),
            pltpu.VMEM((HIST_S + TM, D), f32),
            pltpu.VMEM((SUBLANES, D), f32),
        ] + [pltpu.VMEM((TM, D), f32) for _ in range(N_SLOTS)],
        compiler_params=pltpu.CompilerParams(
            dimension_semantics=("arbitrary", "arbitrary"), vmem_limit_bytes=VMEM_LIMIT_BYTES),
        name=f"prompt_layer{layer}",
    )(x, *weights)


class _Refs:
    def __init__(self, vec, w_in, w_a, w_b, w_c, w_o, w_rga, w_rgx, hb, p, q, z, yacc, wexp):
        self.vec, self.w_in, self.w_a, self.w_b, self.w_c, self.w_o = vec, w_in, w_a, w_b, w_c, w_o
        self.w_rga, self.w_rgx = w_rga, w_rgx
        self.hb, self.p, self.q, self.z, self.yacc, self.wexp = hb, p, q, z, yacc, wexp

    def row(self, i):
        return self.vec[i:i + 1, :]

    def bias(self, g):
        return self.vec[V_B_IN + g:V_B_IN + g + 1, :]


def _in_proj(r, n_rows, g, dst_g):
    r.p[0:n_rows, _cols(dst_g)] = jnp.dot(
        r.hb[0:n_rows, :], r.w_in[:, _cols(g)], preferred_element_type=jnp.float32)


def _layer_body(r, n_rows, x_load, y_store, glu_store, conv_a, bx_store, conv_b, scan, cc_store, conv_c):
    f32, bf16 = jnp.float32, jnp.bfloat16

    g_pre = r.row(V_NORM_PRE)

    def norm_body(r0):
        r.hb[pl.ds(r0, RC), :] = (_rms_scale(x_load(r0)) * g_pre).astype(bf16)
    _for_chunks(n_rows, RC, norm_body)

    def merge(first, last):
        def body(r0, bias):
            rows = pl.ds(r0, RC)
            contrib = _sigmoid(r.p[rows, _cols(1)] + bias) * r.p[rows, _cols(0)]
            if not first:
                contrib = r.yacc[rows, :] + contrib
            if last:
                r.z[rows, :] = contrib.astype(bf16)
            else:
                r.yacc[rows, :] = contrib
        return body

    def branch_out(w_ref, gate_g, first=False, last=False):
        r.p[0:n_rows, _cols(0)] = jnp.dot(r.z[0:n_rows, :], w_ref[...], preferred_element_type=f32)
        _in_proj(r, n_rows, gate_g, 1)
        bias = r.bias(gate_g)
        body = merge(first, last)
        _for_chunks(n_rows, RC, lambda r0: body(r0, bias))

    for j, g in enumerate((C_AV, C_AG, C_AZ)):
        _in_proj(r, n_rows, g, j)
    b_av, b_ag, b_az = r.bias(C_AV), r.bias(C_AG), r.bias(C_AZ)

    def glu_body(r0):
        rows = pl.ds(r0, RC)
        glu_store(r0, (r.p[rows, _cols(0)] + b_av) * _sigmoid(r.p[rows, _cols(1)] + b_ag))
    _for_chunks(n_rows, RC, glu_body)

    conv_a()
    conv_a_b, ln_g, ln_b = r.row(V_CONV_A_B), r.row(V_LN_G), r.row(V_LN_B)

    def ln_body(r0):
        rows = pl.ds(r0, RC)
        y = _layer_norm(r.p[rows, _cols(0)] + conv_a_b, ln_g, ln_b)
        r.z[rows, :] = (_silu(y) * _silu(r.p[rows, _cols(2)] + b_az)).astype(bf16)
    _for_chunks(n_rows, RC, ln_body)
    branch_out(r.w_a, C_G0, first=True)

    for j, g in enumerate((C_BX, C_BZ)):
        _in_proj(r, n_rows, g, j)
    b_bx, b_bz = r.bias(C_BX), r.bias(C_BZ)

    def bx_body(r0):
        bx_store(r0, r.p[pl.ds(r0, RC), _cols(0)] + b_bx)
    _for_chunks(n_rows, RC, bx_body)

    conv_b()

    def xb_cast_body(r0):
        rows = pl.ds(r0, RC)
        r.z[rows, :] = r.p[rows, _cols(2)].astype(bf16)
    _for_chunks(n_rows, RC, xb_cast_body)

    for g in range(N_RG_GROUPS):
        lanes = slice(g * RG_W, (g + 1) * RG_W)
        xg = r.z[0:n_rows, lanes]
        r.q[0:n_rows, lanes] = jnp.dot(xg, r.w_rga[g], preferred_element_type=f32)
        r.q[0:n_rows, slice(D + g * RG_W, D + (g + 1) * RG_W)] = jnp.dot(
            xg, r.w_rgx[g], preferred_element_type=f32)

    softplus_neg_lam = _softplus(-r.row(V_LAMBDA))
    rg_b_a, rg_b_x = r.row(V_RG_B_A), r.row(V_RG_B_X)

    def rg_body(r0):
        rows = pl.ds(r0, RC)
        a, u = _lru_coeffs(_sigmoid(r.q[rows, _cols(0)] + rg_b_a), _sigmoid(r.q[rows, _cols(1)] + rg_b_x),
                           softplus_neg_lam, r.p[rows, _cols(2)])
        r.q[rows, _cols(0)] = a
        r.q[rows, _cols(1)] = u
    _for_chunks(n_rows, RC, rg_body)

    scan()

    def bz_body(r0):
        rows = pl.ds(r0, RC)
        r.z[rows, :] = (r.q[rows, _cols(0)] * _silu(r.p[rows, _cols(1)] + b_bz)).astype(bf16)
    _for_chunks(n_rows, RC, bz_body)
    branch_out(r.w_b, C_G1)

    for j, g in enumerate((C_CB, C_CC, C_CX, C_CZ)):
        _in_proj(r, n_rows, g, j)
    b_cb, b_cc, b_cx, b_cz = r.bias(C_CB), r.bias(C_CC), r.bias(C_CX), r.bias(C_CZ)

    def cc_body(r0):
        rows = pl.ds(r0, RC)
        cc_store(r0, (r.p[rows, _cols(1)] + b_cc) * (r.p[rows, _cols(2)] + b_cx))
    _for_chunks(n_rows, RC, cc_body)

    conv_c()

    def cz_body(r0):
        rows = pl.ds(r0, RC)
        zc = ((r.p[rows, _cols(0)] + b_cb) * r.p[rows, _cols(1)]) * _silu(r.p[rows, _cols(3)] + b_cz)
        r.z[rows, :] = zc.astype(bf16)
    _for_chunks(n_rows, RC, cz_body)
    branch_out(r.w_c, C_G2, last=True)

    r.p[0:n_rows, _cols(0)] = jnp.dot(r.z[0:n_rows, :], r.w_o[...], preferred_element_type=f32)
    g_post = r.row(V_NORM_POST)

    def out_body(r0):
        y_store(r0, x_load(r0) + _rms_scale(r.p[pl.ds(r0, RC), _cols(0)]) * g_post)
    _for_chunks(n_rows, RC, out_body)


def _sample_kernel(n_t, x_ref, sa_ref, sb_ref, sh_ref, sc_ref,
                   vec_ref, w_in_ref, w_a_ref, w_b_ref, w_c_ref, w_o_ref, w_rga_ref, w_rgx_ref,
                   y_ref, ua_out_ref, nb_ref, nh_ref, nc_ref,
                   hb_ref, p_ref, q_ref, z_ref, yacc_ref, wexp_ref, ua_ref, bx_ref, cc_ref):
    bs = SAMPLE_BLOCK
    n_rows = n_t * bs
    r = _Refs(vec_ref, w_in_ref, w_a_ref, w_b_ref, w_c_ref, w_o_ref, w_rga_ref, w_rgx_ref,
              hb_ref, p_ref, q_ref, z_ref, yacc_ref, wexp_ref)
    _expand_conv_a_weights(vec_ref, wexp_ref)

    def flat_store(ref):
        def store(r0, val):
            ref[pl.ds(r0, RC), :] = val
        return store

    def slab_conv(state_ref, new_ref, k_width, tap_weight, dst_g, bias_row):
        n_state = k_width - 1

        def run():
            def body(r0):
                rows = pl.ds(r0, SUBLANES)
                for t in range(n_t):
                    acc = None
                    for k in range(k_width):
                        i = t + k
                        if i < n_state:
                            full = state_ref[i, rows, :]
                        else:
                            full = new_ref[pl.ds((i - n_state) * bs + r0, SUBLANES), :]
                        term = tap_weight(k) * full
                        acc = term if acc is None else acc + term
                    if bias_row is not None:
                        acc = acc + r.row(bias_row)
                    p_ref[pl.ds(t * bs + r0, SUBLANES), _cols(dst_g)] = acc
            _for_chunks(bs, SUBLANES, body)
        return run

    def vec_tap(v_row):
        return lambda k: r.row(v_row + k)

    def scan():
        def body(r0):
            rows = pl.ds(r0, SUBLANES)
            h = sh_ref[rows, :]
            for t in range(n_t):
                trows = pl.ds(t * bs + r0, SUBLANES)
                h = q_ref[trows, _cols(0)] * h + q_ref[trows, _cols(1)]
                q_ref[trows, _cols(0)] = h
            nh_ref[rows, :] = h
        _for_chunks(bs, SUBLANES, body)

    def y_store(r0, val):
        y_ref[pl.ds(r0, RC), :] = val

    _layer_body(
        r, n_rows,
        x_load=lambda r0: x_ref[pl.ds(r0, RC), :],
        y_store=y_store,
        glu_store=flat_store(ua_ref),
        conv_a=slab_conv(sa_ref, ua_ref, KA,
                         lambda k: wexp_ref[k * SUBLANES:(k + 1) * SUBLANES, :], 0, None),
        bx_store=flat_store(bx_ref),
        conv_b=slab_conv(sb_ref, bx_ref, KB, vec_tap(V_CONV_B_W), 2, V_CONV_B_B),
        scan=scan,
        cc_store=flat_store(cc_ref),
        conv_c=slab_conv(sc_ref, cc_ref, KC, vec_tap(V_CONV_C_W), 1, None))

    ua_out_ref[...] = ua_ref[...]
    nb_ref[...] = bx_ref[(n_t - (KB - 1)) * bs:n_t * bs, :]
    nc_ref[...] = cc_ref[(n_t - (KC - 1)) * bs:n_t * bs, :]


def _sample_layer(x_tm, sa_tm, sb_tm, sh, sc_tm, weights, layer):
    n_blk, n_rows, _ = x_tm.shape
    bs = SAMPLE_BLOCK
    n_t = n_rows // bs
    bsz = n_blk * bs
    assert n_t >= KB - 1 and n_t >= KC - 1 and bs % RC == 0
    f32 = jnp.float32
    state = lambda k: pl.BlockSpec((None, k, bs, D), lambda i: (layer, 0, i, 0))
    rows_blk = lambda n: pl.BlockSpec((None, n, D), lambda i: (i, 0, 0))
    out_shape = (
        jax.ShapeDtypeStruct((n_blk, n_rows, D), f32),
        jax.ShapeDtypeStruct((n_blk, n_rows, D), f32),
        jax.ShapeDtypeStruct((n_blk, (KB - 1) * bs, D), f32),
        jax.ShapeDtypeStruct((bsz, D), f32),
        jax.ShapeDtypeStruct((n_blk, (KC - 1) * bs, D), f32),
    )
    return pl.pallas_call(
        functools.partial(_sample_kernel, n_t),
        grid=(n_blk,),
        in_specs=[rows_blk(n_rows), state(KA - 1), state(KB - 1),
                  pl.BlockSpec((None, bs, D), lambda i: (layer, i, 0)), state(KC - 1)]
        + _weight_specs(weights, layer),
        out_specs=(rows_blk(n_rows), rows_blk(n_rows), rows_blk((KB - 1) * bs),
                   pl.BlockSpec((bs, D), lambda i: (i, 0)), rows_blk((KC - 1) * bs)),
        out_shape=out_shape,
        scratch_shapes=[
            pltpu.VMEM((n_rows, D), jnp.bfloat16),
            pltpu.VMEM((n_rows, 4 * D), f32),
            pltpu.VMEM((n_rows, 2 * D), f32),
            pltpu.VMEM((n_rows, D), jnp.bfloat16),
            pltpu.VMEM((n_rows, D), f32),
            pltpu.VMEM((KA * SUBLANES, D), f32),
            pltpu.VMEM((n_rows, D), f32),
            pltpu.VMEM((n_rows, D), f32),
            pltpu.VMEM((n_rows, D), f32),
        ],
        compiler_params=pltpu.CompilerParams(
            dimension_semantics=("arbitrary",), vmem_limit_bytes=VMEM_LIMIT_BYTES),
        name=f"sample_layer{layer}",
    )(x_tm, sa_tm, sb_tm, sh, sc_tm, *weights)


def _block_diag_groups(w):
    depth, n_blocks, blk, _ = w.shape
    per = n_blocks // N_RG_GROUPS
    w = w.reshape(depth, N_RG_GROUPS, per, blk, blk)
    eye = jnp.eye(per, dtype=w.dtype)
    dense = jnp.einsum('lgaij,ab->lgaibj', w, eye)
    return dense.reshape(depth, N_RG_GROUPS, per * blk, per * blk)


def _to_time_major_blocks(x):
    bsz, n_t, _ = x.shape
    bs = SAMPLE_BLOCK
    return x.reshape(bsz // bs, bs, n_t, D).transpose(0, 2, 1, 3).reshape(bsz // bs, n_t * bs, D)


def _from_time_major_blocks(x, n_t):
    n_blk = x.shape[0]
    bs = SAMPLE_BLOCK
    return x.reshape(n_blk, n_t, bs, D).transpose(0, 2, 1, 3).reshape(n_blk * bs, n_t, D)


def kernel(x_prompt, x_sample, state_conv_a, state_conv_b, state_lru, state_conv_c, norm_pre, norm_post, w_in, b_in, conv_a_w, conv_a_b, ln_a_g, ln_a_b, w_a_out, conv_b_w, conv_b_b, rg_w_a, rg_b_a, rg_w_x, rg_b_x, rg_lambda, w_b_out, conv_c_w, w_c_out, w_o):
    bf16 = jnp.bfloat16
    depth = w_in.shape[0]
    n_t = x_sample.shape[1]

    rows = [norm_pre, norm_post, conv_a_b, ln_a_g, ln_a_b, conv_b_b, rg_b_a, rg_b_x, rg_lambda]
    vec = jnp.concatenate(
        [v[:, None, :] for v in rows]
        + [b_in.reshape(depth, N_COL_GROUPS, D), conv_a_w, conv_b_w, conv_c_w], axis=1)
    vec = jnp.pad(vec, ((0, 0), (0, NV - vec.shape[1]), (0, 0)))
    weights = (vec, w_in.astype(bf16), w_a_out.astype(bf16), w_b_out.astype(bf16),
               w_c_out.astype(bf16), w_o.astype(bf16),
               _block_diag_groups(rg_w_a).astype(bf16), _block_diag_groups(rg_w_x).astype(bf16))

    xs = _to_time_major_blocks(x_sample)
    sa_tm = state_conv_a.transpose(0, 2, 1, 3)
    sb_tm = state_conv_b.transpose(0, 2, 1, 3)
    sc_tm = state_conv_c.transpose(0, 2, 1, 3)

    xp = x_prompt
    pa, pb, ph, pc = [], [], [], []
    sa, sb, sh, sc = [], [], [], []
    for l in range(depth):
        xp, na, nb, nh, nc = _prompt_layer(xp, weights, l)
        pa.append(na); pb.append(nb); ph.append(nh[:, 0, :]); pc.append(nc)

        xs, ua_new, nb, nh, nc = _sample_layer(xs, sa_tm, sb_tm, state_lru, sc_tm, weights, l)
        ua_new = _from_time_major_blocks(ua_new, n_t)
        sa.append(jnp.concatenate([state_conv_a[l], ua_new], axis=1)[:, -(KA - 1):])
        sb.append(_from_time_major_blocks(nb, KB - 1))
        sh.append(nh)
        sc.append(_from_time_major_blocks(nc, KC - 1))

    return (xp, _from_time_major_blocks(xs, n_t),
            jnp.stack(pa), jnp.stack(pb), jnp.stack(ph), jnp.stack(pc),
            jnp.stack(sa), jnp.stack(sb), jnp.stack(sh), jnp.stack(sc))
```
